```python
import math
import jax, jax.numpy as jnp
from jax import lax
import numpy as np

D_MODEL = 1024
BATCH = 8
SEQ = 4096
DEPTH = 4

CHUNK = 64
N_MEM = 256
Q_BLOCK = 128
RMS_EPS = 1e-6

S5_WIDTH = 512
S5_GROUP = 16
S5_GROUPS = S5_WIDTH // S5_GROUP
S5_STATE = 64
S5_DT_MIN = 1e-3
S5_DT_MAX = 1e-1

MLA_HEADS = 8
MLA_NOPE = 64
MLA_ROPE = 32
MLA_QK = MLA_NOPE + MLA_ROPE
MLA_V = 64
MLA_WIDTH = MLA_HEADS * MLA_V
MLA_Q_LORA = 256
MLA_KV_LORA = 128
ROPE_BASE = 10000.0

M2_INNER = 2 * D_MODEL
M2_HEADDIM = 64
M2_HEADS = M2_INNER // M2_HEADDIM
M2_GROUPS = 4
M2_STATE = 128
M2_CONV = 4
M2_CHUNK = 128
M2_CONV_DIM = M2_INNER + 2 * M2_GROUPS * M2_STATE
M2_DT_MIN = 1e-3
M2_DT_MAX = 1e-1

MEM_HEADS = 4
MEM_HD = 128
MEM_WIDTH = MEM_HEADS * MEM_HD

N_EVEN = (DEPTH + 1) // 2
N_ODD = DEPTH // 2

EVEN_SPLITS = (S5_WIDTH, S5_WIDTH, MLA_Q_LORA, MLA_KV_LORA, MLA_ROPE, MLA_WIDTH, MEM_WIDTH, MEM_WIDTH)
EVEN_IN = sum(EVEN_SPLITS)
EVEN_OUT = S5_WIDTH + MLA_WIDTH + MEM_WIDTH
ODD_SPLITS = (M2_INNER, M2_CONV_DIM, M2_HEADS, MEM_WIDTH, MEM_WIDTH)
ODD_IN = sum(ODD_SPLITS)
ODD_OUT = M2_INNER + MEM_WIDTH

kernel_name = "hybrid_s5_mla_ssd_memory_trunk"


def _split(t, sizes):
    idx = [int(i) for i in np.cumsum(sizes)[:-1]]
    return jnp.split(t, idx, axis=-1)


def rmsnorm(x, g):
    xf = x.astype(jnp.float32)
    y = xf * lax.rsqrt(jnp.mean(xf * xf, axis=-1, keepdims=True) + RMS_EPS)
    return (y * g.astype(jnp.float32)).astype(x.dtype)


def rope(x, positions):
    half = x.shape[-1] // 2
    inv = ROPE_BASE ** (-jnp.arange(half, dtype=jnp.float32) / half)
    ang = positions.astype(jnp.float32)[:, :, None, None] * inv
    c, s = jnp.cos(ang), jnp.sin(ang)
    xf = x.astype(jnp.float32)
    x1, x2 = xf[..., :half], xf[..., half:]
    return jnp.concatenate([x1 * c - x2 * s, x1 * s + x2 * c], axis=-1).astype(x.dtype)


def block_causal_attention(q, k, v):
    b, s, h, dk = q.shape
    nb = s // Q_BLOCK
    scale = 1.0 / math.sqrt(dk)
    key_chunk = jnp.arange(s) // CHUNK
    qb = jnp.moveaxis(q.reshape(b, nb, Q_BLOCK, h, dk), 1, 0)

    def one_block(args):
        q_blk, blk = args
        q_chunk = (blk * Q_BLOCK + jnp.arange(Q_BLOCK)) // CHUNK
        sc = jnp.einsum("bqhd,bkhd->bhqk", q_blk, k).astype(jnp.float32) * scale
        sc = jnp.where(key_chunk[None, :] <= q_chunk[:, None], sc, -jnp.inf)
        p = jax.nn.softmax(sc, axis=-1).astype(v.dtype)
        return jnp.einsum("bhqk,bkhd->bqhd", p, v)

    out = lax.map(one_block, (qb, jnp.arange(nb)))
    return jnp.moveaxis(out, 0, 1).reshape(b, s, h, v.shape[-1])


def s5_mixer(u, a_re, a_im, log_dt, b_re, b_im, c_re, c_im, d, glu_w, glu_b):
    f32 = jnp.float32
    bsz, s, _ = u.shape
    dt = jnp.exp(log_dt.astype(f32))[:, None]
    ar, ai = a_re.astype(f32), a_im.astype(f32)
    mag = jnp.exp(ar * dt)
    ab_re, ab_im = mag * jnp.cos(ai * dt), mag * jnp.sin(ai * dt)
    den = ar * ar + ai * ai
    n_re, n_im = ab_re - 1.0, ab_im
    f_re = (n_re * ar + n_im * ai) / den
    f_im = (n_im * ar - n_re * ai) / den
    br, bi = b_re.astype(f32), b_im.astype(f32)
    bb_re = f_re[..., None] * br - f_im[..., None] * bi
    bb_im = f_re[..., None] * bi + f_im[..., None] * br
    uf = u.astype(f32)
    ug = uf.reshape(bsz, s, S5_GROUPS, S5_GROUP)
    bu_re = jnp.einsum("bsgc,gpc->sbgp", ug, bb_re)
    bu_im = jnp.einsum("bsgc,gpc->sbgp", ug, bb_im)
    a_seq_re = jnp.broadcast_to(ab_re, (s, 1, S5_GROUPS, S5_STATE))
    a_seq_im = jnp.broadcast_to(ab_im, (s, 1, S5_GROUPS, S5_STATE))

    def combine(l, r):
        la_re, la_im, lb_re, lb_im = l
        ra_re, ra_im, rb_re, rb_im = r
        return (la_re * ra_re - la_im * ra_im,
                la_re * ra_im + la_im * ra_re,
                ra_re * lb_re - ra_im * lb_im + rb_re,
                ra_re * lb_im + ra_im * lb_re + rb_im)

    _, _, h_re, h_im = lax.associative_scan(combine, (a_seq_re, a_seq_im, bu_re, bu_im), axis=0)
    y = (jnp.einsum("sbgp,gcp->bsgc", h_re, c_re.astype(f32))
         - jnp.einsum("sbgp,gcp->bsgc", h_im, c_im.astype(f32)))
    y = y.reshape(bsz, s, S5_WIDTH) + d.astype(f32) * uf
    y = jax.nn.gelu(y)
    y = y * jax.nn.sigmoid(y @ glu_w.astype(f32) + glu_b.astype(f32))
    return y.astype(u.dtype)


def mla_mixer(c_q, c_kv, k_rope, positions, q_a_norm_g, w_uq, kv_a_norm_g, w_ukv, q_norm_g, k_norm_g):
    bsz, s, _ = c_q.shape
    q = (rmsnorm(c_q, q_a_norm_g) @ w_uq).reshape(bsz, s, MLA_HEADS, MLA_QK)
    kv = (rmsnorm(c_kv, kv_a_norm_g) @ w_ukv).reshape(bsz, s, MLA_HEADS, MLA_NOPE + MLA_V)
    k_nope, v = kv[..., :MLA_NOPE], kv[..., MLA_NOPE:]
    k_r = jnp.broadcast_to(k_rope[:, :, None, :], (bsz, s, MLA_HEADS, MLA_ROPE))
    k = jnp.concatenate([k_nope, k_r], axis=-1)
    q = rmsnorm(q, q_norm_g)
    k = rmsnorm(k, k_norm_g)
    q = jnp.concatenate([q[..., :MLA_NOPE], rope(q[..., MLA_NOPE:], positions)], axis=-1)
    k = jnp.concatenate([k[..., :MLA_NOPE], rope(k[..., MLA_NOPE:], positions)], axis=-1)
    return block_causal_attention(q, k, v).reshape(bsz, s, MLA_WIDTH)


def memory_attention(q, mem_n, w_kv, q_norm_g, k_norm_g):
    bsz, s, _ = q.shape
    kv = mem_n @ w_kv
    k, v = kv[..., :MEM_WIDTH], kv[..., MEM_WIDTH:]
    q = rmsnorm(q.reshape(bsz, s, MEM_HEADS, MEM_HD), q_norm_g)
    k = rmsnorm(k.reshape(bsz, N_MEM, MEM_HEADS, MEM_HD), k_norm_g)
    v = v.reshape(bsz, N_MEM, MEM_HEADS, MEM_HD)
    sc = jnp.einsum("bqhd,bkhd->bhqk", q, k).astype(jnp.float32) * (1.0 / math.sqrt(MEM_HD))
    p = jax.nn.softmax(sc, axis=-1).astype(v.dtype)
    return jnp.einsum("bhqk,bkhd->bqhd", p, v).reshape(bsz, s, MEM_WIDTH)


def causal_depthwise_conv(x, w, b):
    out = lax.conv_general_dilated(x, w[:, None, :], window_strides=(1,),
                                   padding=((M2_CONV - 1, 0),),
                                   dimension_numbers=("NWC", "WIO", "NWC"),
                                   feature_group_count=x.shape[-1])
    return out + b


def ssd_scan(xh, dt, a, bm, cm):
    f32 = jnp.float32
    bsz, s, nh, p = xh.shape
    nc, L, g = s // M2_CHUNK, M2_CHUNK, M2_GROUPS
    hg = nh // g
    x = (xh.astype(f32) * dt[..., None]).reshape(bsz, nc, L, g, hg, p)
    da = (dt * a).reshape(bsz, nc, L, g, hg)
    bm = bm.astype(f32).reshape(bsz, nc, L, g, M2_STATE)
    cm = cm.astype(f32).reshape(bsz, nc, L, g, M2_STATE)
    cs = jnp.cumsum(da, axis=2)
    tril = jnp.tril(jnp.ones((L, L), dtype=bool))
    diff = cs[:, :, :, None] - cs[:, :, None, :]
    seg = jnp.exp(jnp.where(tril[:, :, None, None], diff, -jnp.inf))
    cb = jnp.einsum("bctgn,bcsgn->bctsg", cm, bm)
    y_diag = jnp.einsum("bctsgh,bcsghp->bctghp", cb[..., None] * seg, x)
    xw = x * jnp.exp(cs[:, :, -1:] - cs)[..., None]
    chunk_states = jnp.einsum("bcsgn,bcsghp->bcghpn", bm, xw)
    chunk_decay = jnp.exp(cs[:, :, -1])

    def step(h, inp):
        st, dec = inp
        return h * dec[..., None, None] + st, h

    h0 = jnp.zeros((bsz, g, hg, p, M2_STATE), f32)
    _, h_in = lax.scan(step, h0, (jnp.moveaxis(chunk_states, 1, 0), jnp.moveaxis(chunk_decay, 1, 0)))
    h_in = jnp.moveaxis(h_in, 0, 1)
    y_off = jnp.einsum("bctgn,bcghpn->bctghp", cm, h_in) * jnp.exp(cs)[..., None]
    return (y_diag + y_off).reshape(bsz, s, nh, p)


def mamba2_mixer(z, xbc, dt_raw, conv_w, conv_b, dt_bias, a_log, d, norm_g):
    f32 = jnp.float32
    bsz, s, _ = z.shape
    xbc = jax.nn.silu(causal_depthwise_conv(xbc, conv_w, conv_b))
    xs, bm, cm = _split(xbc, (M2_INNER, M2_GROUPS * M2_STATE, M2_GROUPS * M2_STATE))
    xh = xs.reshape(bsz, s, M2_HEADS, M2_HEADDIM)
    dt = jax.nn.softplus(dt_raw.astype(f32) + dt_bias.astype(f32))
    a = -jnp.exp(a_log.astype(f32))
    y = ssd_scan(xh, dt, a, bm.reshape(bsz, s, M2_GROUPS, M2_STATE), cm.reshape(bsz, s, M2_GROUPS, M2_STATE))
    y = (y + d.astype(f32)[:, None] * xh.astype(f32)).reshape(bsz, s, M2_INNER)
    gated = (y * jax.nn.silu(z.astype(f32))).reshape(bsz, s, M2_GROUPS, M2_INNER // M2_GROUPS)
    gated = gated * lax.rsqrt(jnp.mean(gated * gated, axis=-1, keepdims=True) + RMS_EPS)
    return (gated.reshape(bsz, s, M2_INNER) * norm_g.astype(f32)).astype(z.dtype)


def setup_inputs(seed: int = 0) -> dict:
    key = jax.random.key(seed)
    ks = iter(jax.random.split(key, 64))
    f32 = jnp.float32

    def nrm(shape, scale):
        return scale * jax.random.normal(next(ks), shape, f32)

    def gain(shape):
        return 1.0 + 0.01 * jax.random.normal(next(ks), shape, f32)

    E, O = N_EVEN, N_ODD
    G, P = S5_GROUPS, S5_STATE
    x = nrm((BATCH, SEQ, D_MODEL), 1.0)
    mem = nrm((BATCH, N_MEM, D_MODEL), 1.0)
    offset = CHUNK * jax.random.randint(next(ks), (BATCH, 1), 0, 256, dtype=jnp.int32)
    positions = offset + jnp.arange(SEQ, dtype=jnp.int32)[None, :]
    norm_g = gain((DEPTH, D_MODEL))
    mem_norm_g = gain((DEPTH, D_MODEL))
    mem_w_kv = nrm((DEPTH, D_MODEL, 2 * MEM_WIDTH), D_MODEL ** -0.5)
    mem_q_norm_g = gain((DEPTH, MEM_HD))
    mem_k_norm_g = gain((DEPTH, MEM_HD))
    ev_w_in = nrm((E, D_MODEL, EVEN_IN), D_MODEL ** -0.5)
    ev_w_out = nrm((E, EVEN_OUT, D_MODEL), EVEN_OUT ** -0.5)
    s5_a_re = -0.5 + nrm((E, G, P), 0.01)
    s5_a_im = jnp.pi * jnp.arange(P, dtype=f32) + nrm((E, G, P), 0.01)
    s5_log_dt = jax.random.uniform(next(ks), (E, G), f32, math.log(S5_DT_MIN), math.log(S5_DT_MAX))
    s5_b_re = nrm((E, G, P, S5_GROUP), (2 * S5_GROUP) ** -0.5)
    s5_b_im = nrm((E, G, P, S5_GROUP), (2 * S5_GROUP) ** -0.5)
    s5_c_re = nrm((E, G, S5_GROUP, P), (2 * P) ** -0.5)
    s5_c_im = nrm((E, G, S5_GROUP, P), (2 * P) ** -0.5)
    s5_d = nrm((E, S5_WIDTH), 1.0)
    s5_glu_w = nrm((E, S5_WIDTH, S5_WIDTH), S5_WIDTH ** -0.5)
    s5_glu_b = nrm((E, S5_WIDTH), 0.01)
    mla_q_a_norm_g = gain((E, MLA_Q_LORA))
    mla_w_uq = nrm((E, MLA_Q_LORA, MLA_HEADS * MLA_QK), MLA_Q_LORA ** -0.5)
    mla_kv_a_norm_g = gain((E, MLA_KV_LORA))
    mla_w_ukv = nrm((E, MLA_KV_LORA, MLA_HEADS * (MLA_NOPE + MLA_V)), MLA_KV_LORA ** -0.5)
    mla_q_norm_g = gain((E, MLA_QK))
    mla_k_norm_g = gain((E, MLA_QK))
    od_w_in = nrm((O, D_MODEL, ODD_IN), D_MODEL ** -0.5)
    od_w_out = nrm((O, ODD_OUT, D_MODEL), ODD_OUT ** -0.5)
    m2_conv_w = nrm((O, M2_CONV, M2_CONV_DIM), M2_CONV ** -0.5)
    m2_conv_b = nrm((O, M2_CONV_DIM), 0.01)
    dt0 = jnp.exp(jax.random.uniform(next(ks), (O, M2_HEADS), f32, math.log(M2_DT_MIN), math.log(M2_DT_MAX)))
    m2_dt_bias = dt0 + jnp.log(-jnp.expm1(-dt0))
    m2_a_log = jnp.log(jax.random.uniform(next(ks), (O, M2_HEADS), f32, 1.0, 16.0))
    m2_d = gain((O, M2_HEADS))
    m2_norm_g = gain((O, M2_INNER))
    return {"x": x, "mem": mem, "positions": positions, "norm_g": norm_g,
            "mem_norm_g": mem_norm_g, "mem_w_kv": mem_w_kv, "mem_q_norm_g": mem_q_norm_g,
            "mem_k_norm_g": mem_k_norm_g, "ev_w_in": ev_w_in, "ev_w_out": ev_w_out,
            "s5_a_re": s5_a_re, "s5_a_im": s5_a_im, "s5_log_dt": s5_log_dt,
            "s5_b_re": s5_b_re, "s5_b_im": s5_b_im, "s5_c_re": s5_c_re, "s5_c_im": s5_c_im,
            "s5_d": s5_d, "s5_glu_w": s5_glu_w, "s5_glu_b": s5_glu_b,
            "mla_q_a_norm_g": mla_q_a_norm_g, "mla_w_uq": mla_w_uq,
            "mla_kv_a_norm_g": mla_kv_a_norm_g, "mla_w_ukv": mla_w_ukv,
            "mla_q_norm_g": mla_q_norm_g, "mla_k_norm_g": mla_k_norm_g,
            "od_w_in": od_w_in, "od_w_out": od_w_out, "m2_conv_w": m2_conv_w,
            "m2_conv_b": m2_conv_b, "m2_dt_bias": m2_dt_bias, "m2_a_log": m2_a_log,
            "m2_d": m2_d, "m2_norm_g": m2_norm_g}


def reference(x, mem, positions, norm_g, mem_norm_g, mem_w_kv, mem_q_norm_g, mem_k_norm_g,
              ev_w_in, ev_w_out, s5_a_re, s5_a_im, s5_log_dt, s5_b_re, s5_b_im, s5_c_re, s5_c_im,
              s5_d, s5_glu_w, s5_glu_b, mla_q_a_norm_g, mla_w_uq, mla_kv_a_norm_g, mla_w_ukv,
              mla_q_norm_g, mla_k_norm_g, od_w_in, od_w_out, m2_conv_w, m2_conv_b, m2_dt_bias,
              m2_a_log, m2_d, m2_norm_g):
    h = x
    for layer in range(DEPTH):
        xn = rmsnorm(h, norm_g[layer])
        mem_n = rmsnorm(mem, mem_norm_g[layer])
        if layer % 2 == 0:
            i = layer // 2
            u_a, z_a, c_q, c_kv, k_rope, z_b, q_mem, z_mem = _split(xn @ ev_w_in[i], EVEN_SPLITS)
            y_a = s5_mixer(u_a, s5_a_re[i], s5_a_im[i], s5_log_dt[i], s5_b_re[i], s5_b_im[i],
                           s5_c_re[i], s5_c_im[i], s5_d[i], s5_glu_w[i], s5_glu_b[i]) * jax.nn.silu(z_a)
            y_b = mla_mixer(c_q, c_kv, k_rope, positions, mla_q_a_norm_g[i], mla_w_uq[i],
                            mla_kv_a_norm_g[i], mla_w_ukv[i], mla_q_norm_g[i],
                            mla_k_norm_g[i]) * jax.nn.silu(z_b)
            y_m = memory_attention(q_mem, mem_n, mem_w_kv[layer], mem_q_norm_g[layer],
                                   mem_k_norm_g[layer]) * jax.nn.silu(z_mem)
            h = h + jnp.concatenate([y_a, y_b, y_m], axis=-1) @ ev_w_out[i]
        else:
            i = layer // 2
            z_c, xbc, dt_raw, q_mem, z_mem = _split(xn @ od_w_in[i], ODD_SPLITS)
            y_c = mamba2_mixer(z_c, xbc, dt_raw, m2_conv_w[i], m2_conv_b[i], m2_dt_bias[i],
                               m2_a_log[i], m2_d[i], m2_norm_g[i])
            y_m = memory_attention(q_mem, mem_n, mem_w_kv[layer], mem_q_norm_g[layer],
                                   mem_k_norm_g[layer]) * jax.nn.silu(z_mem)
            h = h + jnp.concatenate([y_c, y_m], axis=-1) @ od_w_out[i]
    return h
```

```python
import functools
import math

import jax
import jax.numpy as jnp
from jax import lax
from jax.experimental import pallas as pl
from jax.experimental.pallas import tpu as pltpu

F32 = jnp.float32
BF16 = jnp.bfloat16

D_MODEL = 1024
DEPTH = 4
CHUNK = 64
N_MEM = 256
RMS_EPS = 1e-6

S5_WIDTH = 512
S5_GROUP = 16
S5_GROUPS = S5_WIDTH // S5_GROUP
S5_STATE = 64
S5_L = 16
S5_FLAT = S5_L * S5_GROUP

MLA_HEADS = 8
MLA_NOPE = 64
MLA_ROPE = 32
MLA_QK = MLA_NOPE + MLA_ROPE
MLA_V = 64
MLA_WIDTH = MLA_HEADS * MLA_V
MLA_Q_LORA = 256
MLA_KV_LORA = 128
ROPE_BASE = 10000.0
ROPE_HALF = MLA_ROPE // 2
LANES = 128

M2_INNER = 2 * D_MODEL
M2_HEADDIM = 64
M2_HEADS = M2_INNER // M2_HEADDIM
M2_GROUPS = 4
M2_STATE = 128
M2_CONV = 4
M2_CHUNK = 128
M2_CONV_DIM = M2_INNER + 2 * M2_GROUPS * M2_STATE
M2_GW = M2_INNER // M2_GROUPS

MEM_HEADS = 4
MEM_HD = 128
MEM_WIDTH = MEM_HEADS * MEM_HD

VMEM_LIMIT = 56 * 1024 * 1024


def _cparams(sem):
    return pltpu.CompilerParams(dimension_semantics=sem, vmem_limit_bytes=VMEM_LIMIT)


def _silu(z):
    return z * jax.nn.sigmoid(z)


def _rms_proj_kernel(h_ref, g_ref, *refs, n_out, col_chunk):
    w_refs, o_refs = refs[:n_out], refs[n_out:]
    x = h_ref[...]
    xn = (x * lax.rsqrt(jnp.mean(x * x, axis=-1, keepdims=True) + RMS_EPS) * g_ref[...]).astype(BF16)
    for w_ref, o_ref in zip(w_refs, o_refs):
        n = w_ref.shape[1]
        for c0 in range(0, n, col_chunk):
            c1 = min(n, c0 + col_chunk)
            o_ref[:, c0:c1] = jnp.dot(xn, w_ref[:, c0:c1], preferred_element_type=F32).astype(o_ref.dtype)


def _rms_proj(h, g, weights, out_dtypes, tm):
    t, d = h.shape
    in_specs = [pl.BlockSpec((tm, d), lambda i: (i, 0)), pl.BlockSpec((1, d), lambda i: (0, 0))]
    in_specs += [pl.BlockSpec(w.shape, lambda i: (0, 0)) for w in weights]
    out_specs = [pl.BlockSpec((tm, w.shape[1]), lambda i: (i, 0)) for w in weights]
    out_shape = [jax.ShapeDtypeStruct((t, w.shape[1]), dt) for w, dt in zip(weights, out_dtypes)]
    return pl.pallas_call(
        functools.partial(_rms_proj_kernel, n_out=len(weights), col_chunk=512),
        grid=(t // tm,), in_specs=in_specs, out_specs=out_specs, out_shape=out_shape,
        compiler_params=_cparams(("parallel",)), name="rms_proj",
    )(h, g.reshape(1, d), *weights)


def _out_proj_kernel(h_ref, *refs, n_in, glu):
    x_refs, w_refs = refs[:n_in], refs[n_in:2 * n_in]
    o_ref = refs[-1]
    acc = h_ref[...]
    for i, (x_ref, w_ref) in enumerate(zip(x_refs, w_refs)):
        x = x_ref[...]
        if glu and i == 0:
            za_ref, gw_ref, gb_ref = refs[2 * n_in:2 * n_in + 3]
            gate = jnp.dot(x, gw_ref[...], preferred_element_type=F32) + gb_ref[...]
            x = (x.astype(F32) * jax.nn.sigmoid(gate) * _silu(za_ref[...].astype(F32))).astype(BF16)
        acc = acc + jnp.dot(x, w_ref[...], preferred_element_type=F32)
    o_ref[...] = acc


def _out_proj(h, xs, ws, tm, glu_args=None):
    t, d = h.shape
    row = lambda i: (i, 0)
    fixed = lambda i: (0, 0)
    in_specs = [pl.BlockSpec((tm, d), row)]
    in_specs += [pl.BlockSpec((tm, x.shape[1]), row) for x in xs]
    in_specs += [pl.BlockSpec(w.shape, fixed) for w in ws]
    args = [h, *xs, *ws]
    if glu_args is not None:
        za, gw, gb = glu_args
        in_specs += [pl.BlockSpec((tm, za.shape[1]), row), pl.BlockSpec(gw.shape, fixed),
                     pl.BlockSpec(gb.shape, fixed)]
        args += [za, gw, gb]
    return pl.pallas_call(
        functools.partial(_out_proj_kernel, n_in=len(xs), glu=glu_args is not None),
        grid=(t // tm,), in_specs=in_specs, out_specs=pl.BlockSpec((tm, d), row),
        out_shape=jax.ShapeDtypeStruct((t, d), F32),
        compiler_params=_cparams(("parallel",)), name="out_proj",
    )(*args)


def _mem_kv_kernel(mem_ref, g_ref, w_ref, kg_ref, k_ref, v_ref):
    x = mem_ref[0]
    xn = (x * lax.rsqrt(jnp.mean(x * x, axis=-1, keepdims=True) + RMS_EPS) * g_ref[0]).astype(BF16)
    kv = jnp.dot(xn, w_ref[0], preferred_element_type=F32)
    for h in range(MEM_HEADS):
        kh = kv[:, h * MEM_HD:(h + 1) * MEM_HD]
        kn = kh * lax.rsqrt(jnp.mean(kh * kh, axis=-1, keepdims=True) + RMS_EPS) * kg_ref[0]
        k_ref[0, 0, :, h * MEM_HD:(h + 1) * MEM_HD] = kn.astype(BF16)
    v_ref[0, 0] = kv[:, MEM_WIDTH:].astype(BF16)


def _mem_kv(mem, mem_norm_g, w_kv, k_norm_g):
    b = mem.shape[0]
    out = jax.ShapeDtypeStruct((DEPTH, b, N_MEM, MEM_WIDTH), BF16)
    return pl.pallas_call(
        _mem_kv_kernel, grid=(DEPTH, b),
        in_specs=[pl.BlockSpec((1, N_MEM, D_MODEL), lambda l, i: (i, 0, 0)),
                  pl.BlockSpec((1, 1, D_MODEL), lambda l, i: (l, 0, 0)),
                  pl.BlockSpec((1, D_MODEL, 2 * MEM_WIDTH), lambda l, i: (l, 0, 0)),
                  pl.BlockSpec((1, 1, MEM_HD), lambda l, i: (l, 0, 0))],
        out_specs=[pl.BlockSpec((1, 1, N_MEM, MEM_WIDTH), lambda l, i: (l, i, 0, 0))] * 2,
        out_shape=[out, out], compiler_params=_cparams(("arbitrary", "arbitrary")), name="mem_kv",
    )(mem, mem_norm_g.reshape(DEPTH, 1, D_MODEL), w_kv.astype(BF16), k_norm_g.reshape(DEPTH, 1, MEM_HD))


def _mem_attn_kernel(q_ref, z_ref, k_ref, v_ref, qg_ref, o_ref):
    scale = 1.0 / math.sqrt(MEM_HD)
    for h in range(MEM_HEADS):
        sl = slice(h * MEM_HD, (h + 1) * MEM_HD)
        q = q_ref[0, :, sl].astype(F32)
        qn = (q * lax.rsqrt(jnp.mean(q * q, axis=-1, keepdims=True) + RMS_EPS) * (qg_ref[...] * scale)).astype(BF16)
        s = lax.dot_general(qn, k_ref[0, 0, :, sl], (((1,), (1,)), ((), ())), preferred_element_type=F32)
        p = jnp.exp(s - jnp.max(s, axis=-1, keepdims=True))
        l = jnp.sum(p, axis=-1, keepdims=True)
        o = jnp.dot(p.astype(BF16), v_ref[0, 0, :, sl], preferred_element_type=F32) / l
        o_ref[0, :, sl] = (o * _silu(z_ref[0, :, sl].astype(F32))).astype(BF16)


def _mem_attn(q, z, k_all, v_all, qg, layer, tq):
    b, s, _ = q.shape
    tok = pl.BlockSpec((1, tq, MEM_WIDTH), lambda i, j: (i, j, 0))
    bank = pl.BlockSpec((1, 1, N_MEM, MEM_WIDTH), lambda i, j: (layer, i, 0, 0))
    return pl.pallas_call(
        _mem_attn_kernel, grid=(b, s // tq),
        in_specs=[tok, tok, bank, bank, pl.BlockSpec((1, MEM_HD), lambda i, j: (0, 0))],
        out_specs=tok, out_shape=jax.ShapeDtypeStruct((b, s, MEM_WIDTH), BF16),
        compiler_params=_cparams(("parallel", "parallel")), name="mem_attn",
    )(q, z, k_all, v_all, qg.reshape(1, MEM_HD))


def _rope_table_kernel(pos_ref, inv_ref, cos_ref, sin_ref):
    ang = pos_ref[0].astype(F32) * inv_ref[...]
    cos_ref[0] = jnp.cos(ang)
    sin_ref[0] = jnp.sin(ang)


def _rope_tables(positions):
    b, s = positions.shape
    inv = ROPE_BASE ** (-jnp.arange(ROPE_HALF, dtype=F32) / ROPE_HALF)
    out = jax.ShapeDtypeStruct((b, ROPE_HALF, s), F32)
    cos, sin = pl.pallas_call(
        _rope_table_kernel, grid=(b,),
        in_specs=[pl.BlockSpec((1, 1, s), lambda i: (i, 0, 0)), pl.BlockSpec((ROPE_HALF, 1), lambda i: (0, 0))],
        out_specs=[pl.BlockSpec((1, ROPE_HALF, s), lambda i: (i, 0, 0))] * 2,
        out_shape=[out, out], compiler_params=_cparams(("parallel",)), name="rope_tables",
    )(positions.reshape(b, 1, s), inv.reshape(ROPE_HALF, 1))
    cos, sin = jnp.swapaxes(cos, 1, 2), jnp.swapaxes(sin, 1, 2)
    tail = LANES - MLA_QK
    cos_t = jnp.concatenate([jnp.ones((b, s, MLA_NOPE), F32), cos, cos, jnp.ones((b, s, tail), F32)], -1)
    sin_t = jnp.concatenate([jnp.zeros((b, s, MLA_NOPE), F32), -sin, sin, jnp.zeros((b, s, tail), F32)], -1)
    return cos_t.reshape(b * s, LANES), sin_t.reshape(b * s, LANES)


def _head_norm_rope(x, gain, cos, sin, lane):
    xn = x * lax.rsqrt(jnp.sum(x * x, axis=-1, keepdims=True) * (1.0 / MLA_QK) + RMS_EPS) * gain
    partner = jnp.where(lane < MLA_NOPE + ROPE_HALF,
                        pltpu.roll(xn, LANES - ROPE_HALF, 1), pltpu.roll(xn, ROPE_HALF, 1))
    return xn * cos + partner * sin


def _mla_prep_kernel(cq_ref, ckv_ref, kr_ref, cos_ref, sin_ref, gqa_ref, wq_ref, gkva_ref, wk_ref, wv_ref,
                     gq_ref, gk_ref, q_ref, k_ref, v_ref):
    cos, sin = cos_ref[...], sin_ref[...]
    lane = lax.broadcasted_iota(jnp.int32, cos.shape, 1)
    cq = cq_ref[...]
    cqn = (cq * lax.rsqrt(jnp.mean(cq * cq, axis=-1, keepdims=True) + RMS_EPS) * gqa_ref[...]).astype(BF16)
    ckv = ckv_ref[...]
    ckvn = (ckv * lax.rsqrt(jnp.mean(ckv * ckv, axis=-1, keepdims=True) + RMS_EPS) * gkva_ref[...]).astype(BF16)
    kr = kr_ref[...]
    qscale = 1.0 / math.sqrt(MLA_QK)
    for h in range(MLA_HEADS):
        sl = slice(h * LANES, (h + 1) * LANES)
        qh = jnp.dot(cqn, wq_ref[:, sl], preferred_element_type=F32)
        q_ref[:, sl] = (_head_norm_rope(qh, gq_ref[...], cos, sin, lane) * qscale).astype(BF16)
        kh = jnp.dot(ckvn, wk_ref[:, sl], preferred_element_type=F32) + kr
        k_ref[:, sl] = _head_norm_rope(kh, gk_ref[...], cos, sin, lane).astype(BF16)
    v_ref[...] = jnp.dot(ckvn, wv_ref[...], preferred_element_type=F32).astype(BF16)


def _pad_heads(w, n_heads, width, offset=0):
    k = w.shape[0]
    w = w.reshape(k, n_heads, width)
    w = jnp.pad(w, ((0, 0), (0, 0), (offset, LANES - width - offset)))
    return w.reshape(k, n_heads * LANES)


def _mla_prep(cq, ckv, kr, cos_t, sin_t, gqa, w_uq, gkva, w_ukv, gq, gk, tm):
    t = cq.shape[0]
    wq = _pad_heads(w_uq, MLA_HEADS, MLA_QK).astype(BF16)
    w_ukv = w_ukv.reshape(MLA_KV_LORA, MLA_HEADS, MLA_NOPE + MLA_V)
    wk = _pad_heads(w_ukv[:, :, :MLA_NOPE].reshape(MLA_KV_LORA, -1), MLA_HEADS, MLA_NOPE).astype(BF16)
    wv = w_ukv[:, :, MLA_NOPE:].reshape(MLA_KV_LORA, MLA_WIDTH).astype(BF16)
    pad_gain = lambda g: jnp.pad(g, (0, LANES - MLA_QK)).reshape(1, LANES)
    row = lambda i: (i, 0)
    fixed = lambda i: (0, 0)
    hw = MLA_HEADS * LANES
    return pl.pallas_call(
        _mla_prep_kernel, grid=(t // tm,),
        in_specs=[pl.BlockSpec((tm, MLA_Q_LORA), row), pl.BlockSpec((tm, MLA_KV_LORA), row),
                  pl.BlockSpec((tm, LANES), row), pl.BlockSpec((tm, LANES), row), pl.BlockSpec((tm, LANES), row),
                  pl.BlockSpec((1, MLA_Q_LORA), fixed), pl.BlockSpec((MLA_Q_LORA, hw), fixed),
                  pl.BlockSpec((1, MLA_KV_LORA), fixed), pl.BlockSpec((MLA_KV_LORA, hw), fixed),
                  pl.BlockSpec((MLA_KV_LORA, MLA_WIDTH), fixed),
                  pl.BlockSpec((1, LANES), fixed), pl.BlockSpec((1, LANES), fixed)],
        out_specs=[pl.BlockSpec((tm, hw), row), pl.BlockSpec((tm, hw), row), pl.BlockSpec((tm, MLA_WIDTH), row)],
        out_shape=[jax.ShapeDtypeStruct((t, hw), BF16), jax.ShapeDtypeStruct((t, hw), BF16),
                   jax.ShapeDtypeStruct((t, MLA_WIDTH), BF16)],
        compiler_params=_cparams(("parallel",)), name="mla_prep",
    )(cq, ckv, kr, cos_t, sin_t, gqa.reshape(1, -1), wq, gkva.reshape(1, -1), wk, wv, pad_gain(gq), pad_gain(gk))


def _mla_attn_kernel(q_ref, k_ref, v_ref, z_ref, o_ref, *, tq):
    qi = pl.program_id(2)
    heads = []
    for h in range(2):
        hs = slice(h * LANES, (h + 1) * LANES)
        q = q_ref[0, :, hs]

        def block(ki, carry, masked):
            m, l, acc = carry
            start = pl.multiple_of(ki * tq, tq)
            k = k_ref[0, pl.ds(start, tq), hs]
            v = v_ref[0, pl.ds(start, tq), :]
            s = lax.dot_general(q, k, (((1,), (1,)), ((), ())), preferred_element_type=F32)
            if masked:
                qc = lax.broadcasted_iota(jnp.int32, s.shape, 0) // CHUNK
                kc = lax.broadcasted_iota(jnp.int32, s.shape, 1) // CHUNK
                s = jnp.where(kc <= qc, s, -jnp.inf)
            m_new = jnp.maximum(m, jnp.max(s, axis=-1, keepdims=True))
            alpha = jnp.exp(m - m_new)
            p = jnp.exp(s - m_new)
            l = alpha * l + jnp.sum(p, axis=-1, keepdims=True)
            acc = alpha * acc + jnp.dot(p.astype(BF16), v, preferred_element_type=F32)
            return m_new, l, acc

        init = (jnp.full((tq, 1), -jnp.inf, F32), jnp.zeros((tq, 1), F32), jnp.zeros((tq, LANES), F32))
        carry = lax.fori_loop(0, qi, functools.partial(block, masked=False), init)
        _, l, acc = block(qi, carry, masked=True)
        heads.append(acc / l)
    lane = lax.broadcasted_iota(jnp.int32, (tq, LANES), 1)
    o = jnp.where(lane < MLA_V, heads[0], heads[1])
    o_ref[0] = (o * _silu(z_ref[0].astype(F32))).astype(BF16)


def _mla_attn(q, k, v, z, tq):
    b, s, _ = q.shape
    pairs = MLA_HEADS // 2
    return pl.pallas_call(
        functools.partial(_mla_attn_kernel, tq=tq), grid=(b, pairs, s // tq),
        in_specs=[pl.BlockSpec((1, tq, 2 * LANES), lambda i, p, j: (i, j, p)),
                  pl.BlockSpec((1, s, 2 * LANES), lambda i, p, j: (i, 0, p)),
                  pl.BlockSpec((1, s, LANES), lambda i, p, j: (i, 0, p)),
                  pl.BlockSpec((1, tq, LANES), lambda i, p, j: (i, j, p))],
        out_specs=pl.BlockSpec((1, tq, LANES), lambda i, p, j: (i, j, p)),
        out_shape=jax.ShapeDtypeStruct((b, s, MLA_WIDTH), BF16),
        compiler_params=_cparams(("parallel", "parallel", "arbitrary")), name="mla_attn",
    )(q, k, v, z)


def _s5_tables(a_re, a_im, log_dt, b_re, b_im, c_re, c_im, n_steps):
    hp = lax.Precision.HIGHEST
    dt = jnp.exp(log_dt)[:, None]
    mag = jnp.exp(a_re * dt)
    ab_re, ab_im = mag * jnp.cos(a_im * dt), mag * jnp.sin(a_im * dt)
    den = a_re * a_re + a_im * a_im
    n_re, n_im = ab_re - 1.0, ab_im
    f_re = (n_re * a_re + n_im * a_im) / den
    f_im = (n_im * a_re - n_re * a_im) / den
    bb_re = f_re[..., None] * b_re - f_im[..., None] * b_im
    bb_im = f_re[..., None] * b_im + f_im[..., None] * b_re

    def lam_pow(k):
        k = k.astype(F32)[:, None, None]
        m = jnp.exp(k * a_re * dt)
        return m * jnp.cos(k * a_im * dt), m * jnp.sin(k * a_im * dt)

    L, C, P, G = S5_L, S5_GROUP, S5_STATE, S5_GROUPS
    lr, li = lam_pow(jnp.arange(L + 1))
    w_re = lr[..., None] * bb_re[None] - li[..., None] * bb_im[None]
    w_im = lr[..., None] * bb_im[None] + li[..., None] * bb_re[None]
    kk = (jnp.einsum("gdp,kgpc->kgdc", c_re, w_re[:L], precision=hp)
          - jnp.einsum("gdp,kgpc->kgdc", c_im, w_im[:L], precision=hp))
    lag = jnp.arange(L)[None, :] - jnp.arange(L)[:, None]
    kfull = jnp.where((lag >= 0)[:, :, None, None, None], kk[jnp.clip(lag, 0, L - 1)], 0.0)
    mt = jnp.transpose(kfull, (2, 0, 4, 1, 3)).reshape(G, L * C, L * C)
    e_re = jnp.transpose(w_re[:L][::-1], (1, 0, 3, 2)).reshape(G, L * C, P)
    e_im = jnp.transpose(w_im[:L][::-1], (1, 0, 3, 2)).reshape(G, L * C, P)
    w1 = jnp.concatenate([mt, e_re, e_im, -e_im, e_re], axis=-1)
    g_re = c_re[None] * lr[1:, :, None, :] - c_im[None] * li[1:, :, None, :]
    g_im = c_re[None] * li[1:, :, None, :] + c_im[None] * lr[1:, :, None, :]
    f_mat = jnp.concatenate([jnp.transpose(g_re, (1, 3, 0, 2)).reshape(G, P, L * C),
                             -jnp.transpose(g_im, (1, 3, 0, 2)).reshape(G, P, L * C)], axis=1)
    sr, si = lam_pow(L * (2 ** jnp.arange(n_steps)))
    la = jnp.transpose(jnp.concatenate([sr, sr], -1), (1, 0, 2))[:, :, None, :]
    lb = jnp.transpose(jnp.concatenate([si, si], -1), (1, 0, 2))[:, :, None, :]
    return w1.astype(BF16), f_mat.astype(BF16), la, lb


def _s5_kernel(u_ref, w1_ref, f_ref, la_ref, lb_ref, d_ref, y_ref, *, gb, n_steps):
    nc = u_ref.shape[2]
    row = lax.broadcasted_iota(jnp.int32, (nc, 2 * S5_STATE), 0)

    def shift(t, d):
        return jnp.where(row >= d, pltpu.roll(t, d, 0), 0.0)

    for g in range(gb):
        u = u_ref[0, g]
        r = jnp.dot(u, w1_ref[g], preferred_element_type=F32)
        y = r[:, :S5_FLAT]
        x = r[:, S5_FLAT:S5_FLAT + 2 * S5_STATE]
        xs = r[:, S5_FLAT + 2 * S5_STATE:]
        for s in range(n_steps):
            a, b = la_ref[g, s], lb_ref[g, s]
            tx, txs = a * x + b * xs, a * xs - b * x
            x, xs = x + shift(tx, 1 << s), xs + shift(txs, 1 << s)
        h_in = shift(x, 1).astype(BF16)
        y = y + jnp.dot(h_in, f_ref[g], preferred_element_type=F32) + d_ref[g] * u.astype(F32)
        y_ref[0, g] = jax.nn.gelu(y, approximate=True).astype(BF16)


def _s5(u, tables, d, gb):
    w1, f_mat, la, lb = tables
    b, s, _ = u.shape
    nc = s // S5_L
    n_steps = la.shape[1]
    G = S5_GROUPS
    uf = u.reshape(b, nc, S5_L, G, S5_GROUP).transpose(0, 3, 1, 2, 4).reshape(b, G, nc, S5_FLAT)
    d_flat = jnp.tile(d.reshape(G, 1, S5_GROUP), (1, 1, S5_L))
    grp = lambda i, j: (j, 0, 0)
    y = pl.pallas_call(
        functools.partial(_s5_kernel, gb=gb, n_steps=n_steps), grid=(b, G // gb),
        in_specs=[pl.BlockSpec((1, gb, nc, S5_FLAT), lambda i, j: (i, j, 0, 0)),
                  pl.BlockSpec((gb, S5_FLAT, 2 * S5_FLAT), grp),
                  pl.BlockSpec((gb, 2 * S5_STATE, S5_FLAT), grp),
                  pl.BlockSpec((gb, n_steps, 1, 2 * S5_STATE), lambda i, j: (j, 0, 0, 0)),
                  pl.BlockSpec((gb, n_steps, 1, 2 * S5_STATE), lambda i, j: (j, 0, 0, 0)),
                  pl.BlockSpec((gb, 1, S5_FLAT), grp)],
        out_specs=pl.BlockSpec((1, gb, nc, S5_FLAT), lambda i, j: (i, j, 0, 0)),
        out_shape=jax.ShapeDtypeStruct((b, G, nc, S5_FLAT), BF16),
        compiler_params=_cparams(("parallel", "parallel")), name="s5",
    )(uf, w1, f_mat, la, lb, d_flat)
    return y.reshape(b, G, nc, S5_L, S5_GROUP).transpose(0, 2, 3, 1, 4).reshape(b, s, S5_WIDTH)


def _split_dot(x, e):
    hi = x.astype(BF16)
    lo = (x - hi.astype(F32)).astype(BF16)
    return jnp.dot(hi, e, preferred_element_type=F32) + jnp.dot(lo, e, preferred_element_type=F32)


def _ssd_kernel(z_ref, xbc_ref, dt_ref, cw_ref, cb_ref, dtb_ref, alog_ref, dexp_ref, ng_ref, e_ref,
                y_ref, xpad, state, gbuf):
    L, PAD = M2_CHUNK, 8

    @pl.when(pl.program_id(1) == 0)
    def _():
        xpad[0:PAD, :] = jnp.zeros((PAD, M2_CONV_DIM), F32)
        state[...] = jnp.zeros_like(state)

    xpad[PAD:PAD + L, :] = xbc_ref[0].astype(F32)
    acc = cb_ref[...] + cw_ref[0:1, :] * xpad[PAD - 3:PAD - 3 + L, :]
    for k in range(1, M2_CONV):
        acc = acc + cw_ref[k:k + 1, :] * xpad[PAD - 3 + k:PAD - 3 + k + L, :]
    xpad[0:PAD, :] = xpad[L:L + PAD, :]
    xc = _silu(acc)
    xs = xc[:, :M2_INNER]
    nbc = M2_GROUPS * M2_STATE
    bm = xc[:, M2_INNER:M2_INNER + nbc]
    cm = xc[:, M2_INNER + nbc:]

    dt = jax.nn.softplus(dt_ref[0] + dtb_ref[...])
    da = dt * (-jnp.exp(alog_ref[...]))
    ti = lax.broadcasted_iota(jnp.int32, (L, L), 0)
    si = lax.broadcasted_iota(jnp.int32, (L, L), 1)
    causal = si <= ti
    tril =jnp.where(causal, 1.0, 0.0).astype(BF16)
    hi = da.astype(BF16)
    r1 = da - hi.astype(F32)
    mid = r1.astype(BF16)
    lo = (r1 - mid.astype(F32)).astype(BF16)
    cs = (jnp.dot(tril, hi, preferred_element_type=F32) + jnp.dot(tril, mid, preferred_element_type=F32)
          + jnp.dot(tril, lo, preferred_element_type=F32))
    cs_t = cs.T
    cs_end = cs[L - 1:L, :]
    stack = jnp.concatenate([dt, dt * jnp.exp(cs_end - cs), jnp.exp(cs),
                             jnp.broadcast_to(jnp.exp(cs_end), (PAD, LANES))], axis=0)
    fac = _split_dot(stack, e_ref[...])
    dt_e, dw_e, ecs_e, dend_e = fac[:L], fac[L:2 * L], fac[2 * L:3 * L], fac[3 * L:3 * L + 1]
    x_dt = (xs * dt_e).astype(BF16)
    x_w = (xs * dw_e).astype(BF16)

    lane = lax.broadcasted_iota(jnp.int32, (L, LANES), 1)
    zero = jnp.zeros((L, LANES), BF16)
    for g in range(M2_GROUPS):
        gs = slice(g * M2_GW, (g + 1) * M2_GW)
        b_g = bm[:, g * M2_STATE:(g + 1) * M2_STATE]
        c_g = cm[:, g * M2_STATE:(g + 1) * M2_STATE].astype(BF16)
        cb = lax.dot_general(c_g, b_g.astype(BF16), (((1,), (1,)), ((), ())), preferred_element_type=F32)
        s_old = state[g]
        y_off = jnp.dot(c_g, s_old.astype(BF16), preferred_element_type=F32) * ecs_e[:, gs]
        state[g] = s_old * dend_e[:, gs] + jnp.dot(b_g.T.astype(BF16), x_w[:, gs], preferred_element_type=F32)
        for pr in range(M2_GW // LANES):
            ps = slice(g * M2_GW + pr * LANES, g * M2_GW + (pr + 1) * LANES)
            xp = x_dt[:, ps]
            y_pair = None
            for hh in range(2):
                h = (g * M2_GW + pr * LANES) // M2_HEADDIM + hh
                seg = jnp.exp(jnp.where(causal, cs[:, h:h + 1] - cs_t[h:h + 1, :], -jnp.inf))
                att = (cb * seg).astype(BF16)
                x_h = jnp.where((lane < M2_HEADDIM) == (hh == 0), xp, zero)
                part = jnp.dot(att, x_h, preferred_element_type=F32)
                y_pair = part if y_pair is None else y_pair + part
            y = y_pair + y_off[:, pr * LANES:(pr + 1) * LANES] + dexp_ref[:, ps] * xs[:, ps]
            gbuf[:, ps] = y * _silu(z_ref[0, :, ps].astype(F32))
    for g in range(M2_GROUPS):
        gs = slice(g * M2_GW, (g + 1) * M2_GW)
        gated = gbuf[:, gs]
        y_ref[0, :, gs] = (gated * lax.rsqrt(jnp.mean(gated * gated, axis=-1, keepdims=True) + RMS_EPS)
                           * ng_ref[:, gs]).astype(y_ref.dtype)


def _ssd(z, xbc, dt_raw, conv_w, conv_b, dt_bias, a_log, d, norm_g):
    b, s, _ = z.shape
    pad_row = lambda v: jnp.pad(v, (0, LANES - M2_HEADS)).reshape(1, LANES)
    expand = (jnp.arange(LANES)[:, None] == (jnp.arange(M2_INNER) // M2_HEADDIM)[None, :]).astype(BF16)
    tok = lambda w: pl.BlockSpec((1, M2_CHUNK, w), lambda i, j: (i, j, 0))
    fixed = lambda shape: pl.BlockSpec(shape, lambda i, j: (0, 0))
    return pl.pallas_call(
        _ssd_kernel, grid=(b, s // M2_CHUNK),
        in_specs=[tok(M2_INNER), tok(M2_CONV_DIM), tok(LANES),
                  fixed((M2_CONV, M2_CONV_DIM)), fixed((1, M2_CONV_DIM)), fixed((1, LANES)), fixed((1, LANES)),
                  fixed((1, M2_INNER)), fixed((1, M2_INNER)), fixed((LANES, M2_INNER))],
        out_specs=tok(M2_INNER), out_shape=jax.ShapeDtypeStruct((b, s, M2_INNER), BF16),
        scratch_shapes=[pltpu.VMEM((M2_CHUNK + 8, M2_CONV_DIM), F32),
                        pltpu.VMEM((M2_GROUPS, M2_STATE, M2_GW), F32),
                        pltpu.VMEM((M2_CHUNK, M2_INNER), F32)],
        compiler_params=_cparams(("parallel", "arbitrary")), name="ssd",
    )(z, xbc, dt_raw, conv_w, conv_b.reshape(1, -1), pad_row(dt_bias), pad_row(a_log),
      jnp.repeat(d, M2_HEADDIM).reshape(1, M2_INNER), norm_g.reshape(1, M2_INNER), expand)


def _cols(w, sizes):
    idx, out = 0, []
    for n in sizes:
        out.append(w[:, idx:idx + n])
        idx += n
    return out


def _even_layer(h, b, s, i, layer, p, mem_k, mem_v, rope_t, tm, tq):
    w_u, w_za, w_cq, w_ckv, w_kr, w_zb, w_qm, w_zm = _cols(
        p["ev_w_in"][i], (S5_WIDTH, S5_WIDTH, MLA_Q_LORA, MLA_KV_LORA, MLA_ROPE, MLA_WIDTH, MEM_WIDTH, MEM_WIDTH))
    w_kr = jnp.pad(w_kr, ((0, 0), (MLA_NOPE, LANES - MLA_QK)))
    weights = [w.astype(BF16) for w in (w_u, w_za, w_zb, w_qm, w_zm, w_cq, w_ckv, w_kr)]
    u, za, zb, qm, zm, cq, ckv, kr = _rms_proj(h, p["norm_g"][layer], weights, [BF16] * 5 + [F32] * 3, tm)

    n_steps = max(1, (s // S5_L - 1).bit_length())
    tables = _s5_tables(p["s5_a_re"][i], p["s5_a_im"][i], p["s5_log_dt"][i], p["s5_b_re"][i], p["s5_b_im"][i],
                        p["s5_c_re"][i], p["s5_c_im"][i], n_steps)
    y_s5 = _s5(u.reshape(b, s, S5_WIDTH), tables, p["s5_d"][i], gb=8).reshape(b * s, S5_WIDTH)

    q, k, v = _mla_prep(cq, ckv, kr, rope_t[0], rope_t[1], p["mla_q_a_norm_g"][i], p["mla_w_uq"][i],
                        p["mla_kv_a_norm_g"][i], p["mla_w_ukv"][i], p["mla_q_norm_g"][i], p["mla_k_norm_g"][i], tm)
    hw = MLA_HEADS * LANES
    y_b = _mla_attn(q.reshape(b, s, hw), k.reshape(b, s, hw), v.reshape(b, s, MLA_WIDTH),
                    zb.reshape(b, s, MLA_WIDTH), tq).reshape(b * s, MLA_WIDTH)

    y_m = _mem_attn(qm.reshape(b, s, MEM_WIDTH), zm.reshape(b, s, MEM_WIDTH), mem_k, mem_v,
                    p["mem_q_norm_g"][layer], layer, tm).reshape(b * s, MEM_WIDTH)

    w_out = p["ev_w_out"][i].astype(BF16)
    ws = [w_out[:S5_WIDTH], w_out[S5_WIDTH:S5_WIDTH + MLA_WIDTH], w_out[S5_WIDTH + MLA_WIDTH:]]
    glu = (za, p["s5_glu_w"][i].astype(BF16), p["s5_glu_b"][i].reshape(1, S5_WIDTH))
    return _out_proj(h, [y_s5, y_b, y_m], ws, tm, glu)


def _odd_layer(h, b, s, i, layer, p, mem_k, mem_v, tm):
    w_z, w_xbc, w_dt, w_qm, w_zm = _cols(p["od_w_in"][i], (M2_INNER, M2_CONV_DIM, M2_HEADS, MEM_WIDTH, MEM_WIDTH))
    w_dt = jnp.pad(w_dt, ((0, 0), (0, LANES - M2_HEADS)))
    weights = [w.astype(BF16) for w in (w_z, w_xbc, w_qm, w_zm, w_dt)]
    z, xbc, qm, zm, dt_raw = _rms_proj(h, p["norm_g"][layer], weights, [BF16] * 4 + [F32], tm)
    y_c = _ssd(z.reshape(b, s, M2_INNER), xbc.reshape(b, s, M2_CONV_DIM), dt_raw.reshape(b, s, LANES),
               p["m2_conv_w"][i], p["m2_conv_b"][i], p["m2_dt_bias"][i], p["m2_a_log"][i], p["m2_d"][i],
               p["m2_norm_g"][i]).reshape(b * s, M2_INNER)
    y_m = _mem_attn(qm.reshape(b, s, MEM_WIDTH), zm.reshape(b, s, MEM_WIDTH), mem_k, mem_v,
                    p["mem_q_norm_g"][layer], layer, tm).reshape(b * s, MEM_WIDTH)
    w_out = p["od_w_out"][i].astype(BF16)
    return _out_proj(h, [y_c, y_m], [w_out[:M2_INNER], w_out[M2_INNER:]], tm)


def _token_tile(s):
    return 512 if s % 512 == 0 else s


def kernel(x, mem, positions, norm_g, mem_norm_g, mem_w_kv, mem_q_norm_g, mem_k_norm_g, ev_w_in, ev_w_out, s5_a_re, s5_a_im, s5_log_dt, s5_b_re, s5_b_im, s5_c_re, s5_c_im, s5_d, s5_glu_w, s5_glu_b, mla_q_a_norm_g, mla_w_uq, mla_kv_a_norm_g, mla_w_ukv, mla_q_norm_g, mla_k_norm_g, od_w_in, od_w_out, m2_conv_w, m2_conv_b, m2_dt_bias, m2_a_log, m2_d, m2_norm_g):
    p = dict(norm_g=norm_g, mem_q_norm_g=mem_q_norm_g, ev_w_in=ev_w_in, ev_w_out=ev_w_out,
             s5_a_re=s5_a_re, s5_a_im=s5_a_im, s5_log_dt=s5_log_dt, s5_b_re=s5_b_re, s5_b_im=s5_b_im,
             s5_c_re=s5_c_re, s5_c_im=s5_c_im, s5_d=s5_d, s5_glu_w=s5_glu_w, s5_glu_b=s5_glu_b,
             mla_q_a_norm_g=mla_q_a_norm_g, mla_w_uq=mla_w_uq, mla_kv_a_norm_g=mla_kv_a_norm_g,
             mla_w_ukv=mla_w_ukv, mla_q_norm_g=mla_q_norm_g, mla_k_norm_g=mla_k_norm_g,
             od_w_in=od_w_in, od_w_out=od_w_out, m2_conv_w=m2_conv_w, m2_conv_b=m2_conv_b,
             m2_dt_bias=m2_dt_bias, m2_a_log=m2_a_log, m2_d=m2_d, m2_norm_g=m2_norm_g)
    b, s, d = x.shape
    tm = _token_tile(s)
    tq = 256 if s % 256 == 0 else s
    mem_k, mem_v = _mem_kv(mem, mem_norm_g, mem_w_kv, mem_k_norm_g)
    rope_t = _rope_tables(positions)
    h = x.reshape(b * s, d)
    for layer in range(DEPTH):
        if layer % 2 == 0:
            h = _even_layer(h, b, s, layer // 2, layer, p, mem_k, mem_v, rope_t, tm, tq)
        else:
            h = _odd_layer(h, b, s, layer // 2, layer, p, mem_k, mem_v, tm)
    return h.reshape(b, s, d)
```

```python
import functools
import math

import jax
import jax.numpy as jnp
from jax import lax
from jax.experimental import pallas as pl
from jax.experimental.pallas import tpu as pltpu

F32 = jnp.float32
BF16 = jnp.bfloat16

D_MODEL = 1024
DEPTH = 4
CHUNK = 64
N_MEM = 256
RMS_EPS = 1e-6

S5_WIDTH = 512
S5_GROUP = 16
S5_GROUPS = S5_WIDTH // S5_GROUP
S5_STATE = 64
S5_L = 16
S5_FLAT = S5_L * S5_GROUP

MLA_HEADS = 8
MLA_NOPE = 64
MLA_ROPE = 32
MLA_QK = MLA_NOPE + MLA_ROPE
MLA_V = 64
MLA_WIDTH = MLA_HEADS * MLA_V
MLA_Q_LORA = 256
MLA_KV_LORA = 128
ROPE_BASE = 10000.0
ROPE_HALF = MLA_ROPE // 2
LANES = 128

M2_INNER = 2 * D_MODEL
M2_HEADDIM = 64
M2_HEADS = M2_INNER // M2_HEADDIM
M2_GROUPS = 4
M2_STATE = 128
M2_CONV = 4
M2_CHUNK = 128
M2_CONV_DIM = M2_INNER + 2 * M2_GROUPS * M2_STATE
M2_GW = M2_INNER // M2_GROUPS

MEM_HEADS = 4
MEM_HD = 128
MEM_WIDTH = MEM_HEADS * MEM_HD

VMEM_LIMIT = 56 * 1024 * 1024


def _cparams(sem):
    return pltpu.CompilerParams(dimension_semantics=sem, vmem_limit_bytes=VMEM_LIMIT)


def _silu(z):
    return z * jax.nn.sigmoid(z)


def _rms_proj_kernel(h_ref, g_ref, *refs, n_out, col_chunk):
    w_refs, o_refs = refs[:n_out], refs[n_out:]
    x = h_ref[...]
    xn = (x * lax.rsqrt(jnp.mean(x * x, axis=-1, keepdims=True) + RMS_EPS) * g_ref[...]).astype(BF16)
    for w_ref, o_ref in zip(w_refs, o_refs):
        n = w_ref.shape[1]
        for c0 in range(0, n, col_chunk):
            c1 = min(n, c0 + col_chunk)
            o_ref[:, c0:c1] = jnp.dot(xn, w_ref[:, c0:c1], preferred_element_type=F32).astype(o_ref.dtype)


def _rms_proj(h, g, weights, out_dtypes, tm):
    t, d = h.shape
    in_specs = [pl.BlockSpec((tm, d), lambda i: (i, 0)), pl.BlockSpec((1, d), lambda i: (0, 0))]
    in_specs += [pl.BlockSpec(w.shape, lambda i: (0, 0)) for w in weights]
    out_specs = [pl.BlockSpec((tm, w.shape[1]), lambda i: (i, 0)) for w in weights]
    out_shape = [jax.ShapeDtypeStruct((t, w.shape[1]), dt) for w, dt in zip(weights, out_dtypes)]
    return pl.pallas_call(
        functools.partial(_rms_proj_kernel, n_out=len(weights), col_chunk=512),
        grid=(t // tm,), in_specs=in_specs, out_specs=out_specs, out_shape=out_shape,
        compiler_params=_cparams(("parallel",)), name="rms_proj",
    )(h, g.reshape(1, d), *weights)


def _out_proj_kernel(h_ref, *refs, n_in, glu):
    x_refs, w_refs = refs[:n_in], refs[n_in:2 * n_in]
    o_ref = refs[-1]
    acc = h_ref[...]
    for i, (x_ref, w_ref) in enumerate(zip(x_refs, w_refs)):
        x = x_ref[...]
        if glu and i == 0:
            za_ref, gw_ref, gb_ref = refs[2 * n_in:2 * n_in + 3]
            gate = jnp.dot(x, gw_ref[...], preferred_element_type=F32) + gb_ref[...]
            x = (x.astype(F32) * jax.nn.sigmoid(gate) * _silu(za_ref[...].astype(F32))).astype(BF16)
        acc = acc + jnp.dot(x, w_ref[...], preferred_element_type=F32)
    o_ref[...] = acc


def _out_proj(h, xs, ws, tm, glu_args=None):
    t, d = h.shape
    row = lambda i: (i, 0)
    fixed = lambda i: (0, 0)
    in_specs = [pl.BlockSpec((tm, d), row)]
    in_specs += [pl.BlockSpec((tm, x.shape[1]), row) for x in xs]
    in_specs += [pl.BlockSpec(w.shape, fixed) for w in ws]
    args = [h, *xs, *ws]
    if glu_args is not None:
        za, gw, gb = glu_args
        in_specs += [pl.BlockSpec((tm, za.shape[1]), row), pl.BlockSpec(gw.shape, fixed),
                     pl.BlockSpec(gb.shape, fixed)]
        args += [za, gw, gb]
    return pl.pallas_call(
        functools.partial(_out_proj_kernel, n_in=len(xs), glu=glu_args is not None),
        grid=(t // tm,), in_specs=in_specs, out_specs=pl.BlockSpec((tm, d), row),
        out_shape=jax.ShapeDtypeStruct((t, d), F32),
        compiler_params=_cparams(("parallel",)), name="out_proj",
    )(*args)


def _mem_kv_kernel(mem_ref, g_ref, w_ref, kg_ref, k_ref, v_ref):
    x = mem_ref[0]
    xn = (x * lax.rsqrt(jnp.mean(x * x, axis=-1, keepdims=True) + RMS_EPS) * g_ref[0]).astype(BF16)
    kv = jnp.dot(xn, w_ref[0], preferred_element_type=F32)
    for h in range(MEM_HEADS):
        kh = kv[:, h * MEM_HD:(h + 1) * MEM_HD]
        kn = kh * lax.rsqrt(jnp.mean(kh * kh, axis=-1, keepdims=True) + RMS_EPS) * kg_ref[0]
        k_ref[0, 0, :, h * MEM_HD:(h + 1) * MEM_HD] = kn.astype(BF16)
    v_ref[0, 0] = kv[:, MEM_WIDTH:].astype(BF16)


def _mem_kv(mem, mem_norm_g, w_kv, k_norm_g):
    b = mem.shape[0]
    out = jax.ShapeDtypeStruct((DEPTH, b, N_MEM, MEM_WIDTH), BF16)
    return pl.pallas_call(
        _mem_kv_kernel, grid=(DEPTH, b),
        in_specs=[pl.BlockSpec((1, N_MEM, D_MODEL), lambda l, i: (i, 0, 0)),
                  pl.BlockSpec((1, 1, D_MODEL), lambda l, i: (l, 0, 0)),
                  pl.BlockSpec((1, D_MODEL, 2 * MEM_WIDTH), lambda l, i: (l, 0, 0)),
                  pl.BlockSpec((1, 1, MEM_HD), lambda l, i: (l, 0, 0))],
        out_specs=[pl.BlockSpec((1, 1, N_MEM, MEM_WIDTH), lambda l, i: (l, i, 0, 0))] * 2,
        out_shape=[out, out], compiler_params=_cparams(("arbitrary", "arbitrary")), name="mem_kv",
    )(mem, mem_norm_g.reshape(DEPTH, 1, D_MODEL), w_kv.astype(BF16), k_norm_g.reshape(DEPTH, 1, MEM_HD))


def _mem_attn_kernel(q_ref, z_ref, k_ref, v_ref, qg_ref, o_ref):
    scale = 1.0 / math.sqrt(MEM_HD)
    for h in range(MEM_HEADS):
        sl = slice(h * MEM_HD, (h + 1) * MEM_HD)
        q = q_ref[0, :, sl].astype(F32)
        qn = (q * lax.rsqrt(jnp.mean(q * q, axis=-1, keepdims=True) + RMS_EPS) * (qg_ref[...] * scale)).astype(BF16)
        s = lax.dot_general(qn, k_ref[0, 0, :, sl], (((1,), (1,)), ((), ())), preferred_element_type=F32)
        p = jnp.exp(s - jnp.max(s, axis=-1, keepdims=True))
        l = jnp.sum(p, axis=-1, keepdims=True)
        o = jnp.dot(p.astype(BF16), v_ref[0, 0, :, sl], preferred_element_type=F32) / l
        o_ref[0, :, sl] = (o * _silu(z_ref[0, :, sl].astype(F32))).astype(BF16)


def _mem_attn(q, z, k_all, v_all, qg, layer, tq):
    b, s, _ = q.shape
    tok = pl.BlockSpec((1, tq, MEM_WIDTH), lambda i, j: (i, j, 0))
    bank = pl.BlockSpec((1, 1, N_MEM, MEM_WIDTH), lambda i, j: (layer, i, 0, 0))
    return pl.pallas_call(
        _mem_attn_kernel, grid=(b, s // tq),
        in_specs=[tok, tok, bank, bank, pl.BlockSpec((1, MEM_HD), lambda i, j: (0, 0))],
        out_specs=tok, out_shape=jax.ShapeDtypeStruct((b, s, MEM_WIDTH), BF16),
        compiler_params=_cparams(("parallel", "parallel")), name="mem_attn",
    )(q, z, k_all, v_all, qg.reshape(1, MEM_HD))


def _rope_table_kernel(pos_ref, inv_ref, cos_ref, sin_ref):
    ang = pos_ref[0].astype(F32) * inv_ref[...]
    cos_ref[0] = jnp.cos(ang)
    sin_ref[0] = jnp.sin(ang)


def _rope_tables(positions):
    b, s = positions.shape
    inv = ROPE_BASE ** (-jnp.arange(ROPE_HALF, dtype=F32) / ROPE_HALF)
    out = jax.ShapeDtypeStruct((b, ROPE_HALF, s), F32)
    cos, sin = pl.pallas_call(
        _rope_table_kernel, grid=(b,),
        in_specs=[pl.BlockSpec((1, 1, s), lambda i: (i, 0, 0)), pl.BlockSpec((ROPE_HALF, 1), lambda i: (0, 0))],
        out_specs=[pl.BlockSpec((1, ROPE_HALF, s), lambda i: (i, 0, 0))] * 2,
        out_shape=[out, out], compiler_params=_cparams(("parallel",)), name="rope_tables",
    )(positions.reshape(b, 1, s), inv.reshape(ROPE_HALF, 1))
    tail = LANES - MLA_QK
    cos_q = jnp.concatenate([jnp.ones((b, MLA_NOPE, s), F32), cos, cos, jnp.ones((b, tail, s), F32)], 1)
    sin_q = jnp.concatenate([jnp.zeros((b, MLA_NOPE, s), F32), -sin, sin, jnp.zeros((b, tail, s), F32)], 1)
    cos_k = jnp.swapaxes(cos_q, 1, 2).reshape(b * s, LANES)
    sin_k = jnp.swapaxes(sin_q, 1, 2).reshape(b * s, LANES)
    return cos_q, sin_q, cos_k, sin_k


def _head_norm_rope(x, gain, cos, sin, lane):
    xn = x * lax.rsqrt(jnp.sum(x * x, axis=-1, keepdims=True) * (1.0 / MLA_QK) + RMS_EPS) * gain
    partner = jnp.where(lane < MLA_NOPE + ROPE_HALF,
                        pltpu.roll(xn, LANES - ROPE_HALF, 1), pltpu.roll(xn, ROPE_HALF, 1))
    return xn * cos + partner * sin


def _mla_prep_kernel(cq_ref, ckv_ref, kr_ref, cosq_ref, sinq_ref, cosk_ref, sink_ref, gqa_ref, wqt_ref, gkva_ref,
                     wk_ref, wvt_ref, gq_ref, gk_ref, qt_ref, k_ref, vt_ref):
    cq = cq_ref[...]
    cqn = cq * lax.rsqrt(jnp.mean(cq * cq, axis=-1, keepdims=True) + RMS_EPS) * gqa_ref[...]
    ckv = ckv_ref[...]
    ckvn = ckv * lax.rsqrt(jnp.mean(ckv * ckv, axis=-1, keepdims=True) + RMS_EPS) * gkva_ref[...]
    cqn_t = cqn.T.astype(BF16)
    ckvn_t = ckvn.T.astype(BF16)
    ckvn = ckvn.astype(BF16)
    kr = kr_ref[...]
    cosq, sinq, cosk, sink = cosq_ref[0], sinq_ref[0], cosk_ref[...], sink_ref[...]
    lane = lax.broadcasted_iota(jnp.int32, cosk.shape, 1)
    row = lax.broadcasted_iota(jnp.int32, cosq.shape, 0)
    qscale = math.log2(math.e) / math.sqrt(MLA_QK)
    for h in range(MLA_HEADS):
        sl = slice(h * LANES, (h + 1) * LANES)
        qt = jnp.dot(wqt_ref[sl, :], cqn_t, preferred_element_type=F32)
        qn = qt * lax.rsqrt(jnp.sum(qt * qt, axis=0, keepdims=True) * (1.0 / MLA_QK) + RMS_EPS) * gq_ref[...]
        partner = jnp.where(row < MLA_NOPE + ROPE_HALF,
                            pltpu.roll(qn, LANES - ROPE_HALF, 0), pltpu.roll(qn, ROPE_HALF, 0))
        qt_ref[0, 0, sl, :] = ((qn * cosq + partner * sinq) * qscale).astype(BF16)
        kh = jnp.dot(ckvn, wk_ref[:, sl], preferred_element_type=F32) + kr
        k_ref[:, sl] = _head_norm_rope(kh, gk_ref[...], cosk, sink, lane).astype(BF16)
        vs = slice(h * MLA_V, (h + 1) * MLA_V)
        vt_ref[0, 0, vs, :] = jnp.dot(wvt_ref[vs, :], ckvn_t, preferred_element_type=F32).astype(BF16)


def _pad_heads(w, n_heads, width, offset=0):
    k = w.shape[0]
    w = w.reshape(k, n_heads, width)
    w = jnp.pad(w, ((0, 0), (0, 0), (offset, LANES - width - offset)))
    return w.reshape(k, n_heads * LANES)


def _mla_prep(cq, ckv, kr, rope_t, gqa, w_uq, gkva, w_ukv, gq, gk, b, s, tm):
    cos_q, sin_q, cos_k, sin_k = rope_t
    nt = s // tm
    wqt = _pad_heads(w_uq, MLA_HEADS, MLA_QK).T.astype(BF16)
    w_ukv = w_ukv.reshape(MLA_KV_LORA, MLA_HEADS, MLA_NOPE + MLA_V)
    wk = _pad_heads(w_ukv[:, :, :MLA_NOPE].reshape(MLA_KV_LORA, -1), MLA_HEADS, MLA_NOPE).astype(BF16)
    wvt = w_ukv[:, :, MLA_NOPE:].reshape(MLA_KV_LORA, MLA_WIDTH).T.astype(BF16)
    pad_gain = lambda g: jnp.pad(g, (0, LANES - MLA_QK))
    row = lambda i, j: (i * nt + j, 0)
    fixed = lambda i, j: (0, 0)
    hw = MLA_HEADS * LANES
    return pl.pallas_call(
        _mla_prep_kernel, grid=(b, nt),
        in_specs=[pl.BlockSpec((tm, MLA_Q_LORA), row), pl.BlockSpec((tm, MLA_KV_LORA), row),
                  pl.BlockSpec((tm, LANES), row),
                  pl.BlockSpec((1, LANES, tm), lambda i, j: (i, 0, j)), pl.BlockSpec((1, LANES, tm), lambda i, j: (i, 0, j)),
                  pl.BlockSpec((tm, LANES), row), pl.BlockSpec((tm, LANES), row),
                  pl.BlockSpec((1, MLA_Q_LORA), fixed), pl.BlockSpec((hw, MLA_Q_LORA), fixed),
                  pl.BlockSpec((1, MLA_KV_LORA), fixed), pl.BlockSpec((MLA_KV_LORA, hw), fixed),
                  pl.BlockSpec((MLA_WIDTH, MLA_KV_LORA), fixed),
                  pl.BlockSpec((LANES, 1), fixed), pl.BlockSpec((1, LANES), fixed)],
        out_specs=[pl.BlockSpec((1, 1, hw, tm), lambda i, j: (i, j, 0, 0)), pl.BlockSpec((tm, hw), row),
                   pl.BlockSpec((1, 1, MLA_WIDTH, tm), lambda i, j: (i, j, 0, 0))],
        out_shape=[jax.ShapeDtypeStruct((b, nt, hw, tm), BF16), jax.ShapeDtypeStruct((b * s, hw), BF16),
                   jax.ShapeDtypeStruct((b, nt, MLA_WIDTH, tm), BF16)],
        compiler_params=_cparams(("parallel", "parallel")), name="mla_prep",
    )(cq, ckv, kr, cos_q, sin_q, cos_k, sin_k, gqa.reshape(1, -1), wqt, gkva.reshape(1, -1), wk, wvt,
      pad_gain(gq).reshape(LANES, 1), pad_gain(gk).reshape(1, LANES))


def _mla_attn_kernel(q_ref, k_ref, v_ref, z_ref, o_ref, m_sc, l_sc, acc_sc, *, tq):
    qi = pl.program_id(2)
    m_sc[...] = jnp.full(m_sc.shape, -jnp.inf, F32)
    l_sc[...] = jnp.zeros(l_sc.shape, F32)
    acc_sc[...] = jnp.zeros(acc_sc.shape, F32)

    def block(ki, masked):
        start = pl.multiple_of(ki * tq, tq)
        for h in range(2):
            s = jnp.dot(k_ref[0, pl.ds(start, tq), h * LANES:(h + 1) * LANES], q_ref[0, 0, h * LANES:(h + 1) * LANES, :],
                        preferred_element_type=F32)
            if masked:
                kc = lax.broadcasted_iota(jnp.int32, s.shape, 0) // CHUNK
                qc = lax.broadcasted_iota(jnp.int32, s.shape, 1) // CHUNK
                s = jnp.where(kc <= qc, s, -jnp.inf)
            m_prev = m_sc[h]
            m_new = jnp.maximum(m_prev, jnp.max(s, axis=0, keepdims=True))
            alpha = jnp.exp2(m_prev - m_new)
            p = jnp.exp2(s - m_new)
            l_sc[h] = alpha * l_sc[h] + jnp.sum(p, axis=0, keepdims=True)
            acc_sc[h] = alpha * acc_sc[h] + jnp.dot(v_ref[0, ki, h * MLA_V:(h + 1) * MLA_V, :], p.astype(BF16),
                                                    preferred_element_type=F32)
            m_sc[h] = m_new

    def body(ki, carry):
        block(ki, masked=False)
        return carry

    lax.fori_loop(0, qi, body, 0)
    block(qi, masked=True)
    o_t = jnp.concatenate([acc_sc[0] / l_sc[0], acc_sc[1] / l_sc[1]], axis=0)
    o_ref[0] = (o_t.T * _silu(z_ref[0].astype(F32))).astype(BF16)


def _mla_attn(qt, k, vt, z, tq):
    b, s, _ = k.shape
    nt = s // tq
    pairs = MLA_HEADS // 2
    return pl.pallas_call(
        functools.partial(_mla_attn_kernel, tq=tq), grid=(b, pairs, nt),
        in_specs=[pl.BlockSpec((1, 1, 2 * LANES, tq), lambda i, p, j: (i, j, p, 0)),
                  pl.BlockSpec((1, s, 2 * LANES), lambda i, p, j: (i, 0, p)),
                  pl.BlockSpec((1, nt, 2 * MLA_V, tq), lambda i, p, j: (i, 0, p, 0)),
                  pl.BlockSpec((1, tq, LANES), lambda i, p, j: (i, j, p))],
        out_specs=pl.BlockSpec((1, tq, LANES), lambda i, p, j: (i, j, p)),
        out_shape=jax.ShapeDtypeStruct((b, s, MLA_WIDTH), BF16),
        scratch_shapes=[pltpu.VMEM((2, 1, tq), F32), pltpu.VMEM((2, 1, tq), F32),
                        pltpu.VMEM((2, MLA_V, tq), F32)],
        compiler_params=_cparams(("parallel", "parallel", "arbitrary")), name="mla_attn",
    )(qt, k, vt, z)


def _s5_tables(a_re, a_im, log_dt, b_re, b_im, c_re, c_im, n_steps):
    hp = lax.Precision.HIGHEST
    dt = jnp.exp(log_dt)[:, None]
    mag = jnp.exp(a_re * dt)
    ab_re, ab_im = mag * jnp.cos(a_im * dt), mag * jnp.sin(a_im * dt)
    den = a_re * a_re + a_im * a_im
    n_re, n_im = ab_re - 1.0, ab_im
    f_re = (n_re * a_re + n_im * a_im) / den
    f_im = (n_im * a_re - n_re * a_im) / den
    bb_re = f_re[..., None] * b_re - f_im[..., None] * b_im
    bb_im = f_re[..., None] * b_im + f_im[..., None] * b_re

    def lam_pow(k):
        k = k.astype(F32)[:, None, None]
        m = jnp.exp(k * a_re * dt)
        return m * jnp.cos(k * a_im * dt), m * jnp.sin(k * a_im * dt)

    L, C, P, G = S5_L, S5_GROUP, S5_STATE, S5_GROUPS
    lr, li = lam_pow(jnp.arange(L + 1))
    w_re = lr[..., None] * bb_re[None] - li[..., None] * bb_im[None]
    w_im = lr[..., None] * bb_im[None] + li[..., None] * bb_re[None]
    kk = (jnp.einsum("gdp,kgpc->kgdc", c_re, w_re[:L], precision=hp)
          - jnp.einsum("gdp,kgpc->kgdc", c_im, w_im[:L], precision=hp))
    lag = jnp.arange(L)[None, :] - jnp.arange(L)[:, None]
    kfull = jnp.where((lag >= 0)[:, :, None, None, None], kk[jnp.clip(lag, 0, L - 1)], 0.0)
    mt = jnp.transpose(kfull, (2, 0, 4, 1, 3)).reshape(G, L * C, L * C)
    e_re = jnp.transpose(w_re[:L][::-1], (1, 0, 3, 2)).reshape(G, L * C, P)
    e_im = jnp.transpose(w_im[:L][::-1], (1, 0, 3, 2)).reshape(G, L * C, P)
    w1 = jnp.concatenate([mt, e_re, e_im, -e_im, e_re], axis=-1)
    g_re = c_re[None] * lr[1:, :, None, :] - c_im[None] * li[1:, :, None, :]
    g_im = c_re[None] * li[1:, :, None, :] + c_im[None] * lr[1:, :, None, :]
    f_mat = jnp.concatenate([jnp.transpose(g_re, (1, 3, 0, 2)).reshape(G, P, L * C),
                             -jnp.transpose(g_im, (1, 3, 0, 2)).reshape(G, P, L * C)], axis=1)
    sr, si = lam_pow(L * (2 ** jnp.arange(n_steps)))
    la = jnp.transpose(jnp.concatenate([sr, sr], -1), (1, 0, 2))[:, :, None, :]
    lb = jnp.transpose(jnp.concatenate([si, si], -1), (1, 0, 2))[:, :, None, :]
    return w1.astype(BF16), f_mat.astype(BF16), la, lb


def _s5_kernel(u_ref, w1_ref, f_ref, la_ref, lb_ref, d_ref, y_ref, *, gb, n_steps):
    nc = u_ref.shape[2]
    row = lax.broadcasted_iota(jnp.int32, (nc, 2 * S5_STATE), 0)

    def shift(t, d):
        return jnp.where(row >= d, pltpu.roll(t, d, 0), 0.0)

    for g in range(gb):
        u = u_ref[0, g]
        r = jnp.dot(u, w1_ref[g], preferred_element_type=F32)
        y = r[:, :S5_FLAT]
        x = r[:, S5_FLAT:S5_FLAT + 2 * S5_STATE]
        xs = r[:, S5_FLAT + 2 * S5_STATE:]
        for s in range(n_steps):
            a, b = la_ref[g, s], lb_ref[g, s]
            tx, txs = a * x + b * xs, a * xs - b * x
            x, xs = x + shift(tx, 1 << s), xs + shift(txs, 1 << s)
        h_in = shift(x, 1).astype(BF16)
        y = y + jnp.dot(h_in, f_ref[g], preferred_element_type=F32) + d_ref[g] * u.astype(F32)
        y_ref[0, g] = jax.nn.gelu(y, approximate=True).astype(BF16)


def _s5(u, tables, d, gb):
    w1, f_mat, la, lb = tables
    b, s, _ = u.shape
    nc = s // S5_L
    n_steps = la.shape[1]
    G = S5_GROUPS
    uf = u.reshape(b, nc, S5_L, G, S5_GROUP).transpose(0, 3, 1, 2, 4).reshape(b, G, nc, S5_FLAT)
    d_flat = jnp.tile(d.reshape(G, 1, S5_GROUP), (1, 1, S5_L))
    grp = lambda i, j: (j, 0, 0)
    y = pl.pallas_call(
        functools.partial(_s5_kernel, gb=gb, n_steps=n_steps), grid=(b, G // gb),
        in_specs=[pl.BlockSpec((1, gb, nc, S5_FLAT), lambda i, j: (i, j, 0, 0)),
                  pl.BlockSpec((gb, S5_FLAT, 2 * S5_FLAT), grp),
                  pl.BlockSpec((gb, 2 * S5_STATE, S5_FLAT), grp),
                  pl.BlockSpec((gb, n_steps, 1, 2 * S5_STATE), lambda i, j: (j, 0, 0, 0)),
                  pl.BlockSpec((gb, n_steps, 1, 2 * S5_STATE), lambda i, j: (j, 0, 0, 0)),
                  pl.BlockSpec((gb, 1, S5_FLAT), grp)],
        out_specs=pl.BlockSpec((1, gb, nc, S5_FLAT), lambda i, j: (i, j, 0, 0)),
        out_shape=jax.ShapeDtypeStruct((b, G, nc, S5_FLAT), BF16),
        compiler_params=_cparams(("parallel", "parallel")), name="s5",
    )(uf, w1, f_mat, la, lb, d_flat)
    return y.reshape(b, G, nc, S5_L, S5_GROUP).transpose(0, 2, 3, 1, 4).reshape(b, s, S5_WIDTH)


def _split_dot(x, e):
    hi = x.astype(BF16)
    lo = (x - hi.astype(F32)).astype(BF16)
    return jnp.dot(hi, e, preferred_element_type=F32) + jnp.dot(lo, e, preferred_element_type=F32)


def _ssd_kernel(z_ref, xbc_ref, dt_ref, cw_ref, cb_ref, dtb_ref, alog_ref, dexp_ref, ng_ref, e_ref,
                y_ref, xpad, state, gbuf):
    L, PAD = M2_CHUNK, 8

    @pl.when(pl.program_id(1) == 0)
    def _():
        xpad[0:PAD, :] = jnp.zeros((PAD, M2_CONV_DIM), F32)
        state[...] = jnp.zeros_like(state)

    xpad[PAD:PAD + L, :] = xbc_ref[0].astype(F32)
    acc = cb_ref[...] + cw_ref[0:1, :] * xpad[PAD - 3:PAD - 3 + L, :]
    for k in range(1, M2_CONV):
        acc = acc + cw_ref[k:k + 1, :] * xpad[PAD - 3 + k:PAD - 3 + k + L, :]
    xpad[0:PAD, :] = xpad[L:L + PAD, :]
    xc = _silu(acc)
    xs = xc[:, :M2_INNER]
    nbc = M2_GROUPS * M2_STATE
    bm = xc[:, M2_INNER:M2_INNER + nbc]
    cm = xc[:, M2_INNER + nbc:]

    dt = jax.nn.softplus(dt_ref[0] + dtb_ref[...])
    da = dt * (-jnp.exp(alog_ref[...]))
    ti = lax.broadcasted_iota(jnp.int32, (L, L), 0)
    si = lax.broadcasted_iota(jnp.int32, (L, L), 1)
    causal = si <= ti
    tril =jnp.where(causal, 1.0, 0.0).astype(BF16)
    hi = da.astype(BF16)
    r1 = da - hi.astype(F32)
    mid = r1.astype(BF16)
    lo = (r1 - mid.astype(F32)).astype(BF16)
    cs = (jnp.dot(tril, hi, preferred_element_type=F32) + jnp.dot(tril, mid, preferred_element_type=F32)
          + jnp.dot(tril, lo, preferred_element_type=F32))
    cs_t = cs.T
    cs_end = cs[L - 1:L, :]
    stack = jnp.concatenate([dt, dt * jnp.exp(cs_end - cs), jnp.exp(cs),
                             jnp.broadcast_to(jnp.exp(cs_end), (PAD, LANES))], axis=0)
    fac = _split_dot(stack, e_ref[...])
    dt_e, dw_e, ecs_e, dend_e = fac[:L], fac[L:2 * L], fac[2 * L:3 * L], fac[3 * L:3 * L + 1]
    x_dt = (xs * dt_e).astype(BF16)
    x_w = (xs * dw_e).astype(BF16)

    lane = lax.broadcasted_iota(jnp.int32, (L, LANES), 1)
    zero = jnp.zeros((L, LANES), BF16)
    for g in range(M2_GROUPS):
        gs = slice(g * M2_GW, (g + 1) * M2_GW)
        b_g = bm[:, g * M2_STATE:(g + 1) * M2_STATE]
        c_g = cm[:, g * M2_STATE:(g + 1) * M2_STATE].astype(BF16)
        cb = lax.dot_general(c_g, b_g.astype(BF16), (((1,), (1,)), ((), ())), preferred_element_type=F32)
        s_old = state[g]
        y_off = jnp.dot(c_g, s_old.astype(BF16), preferred_element_type=F32) * ecs_e[:, gs]
        state[g] = s_old * dend_e[:, gs] + jnp.dot(b_g.T.astype(BF16), x_w[:, gs], preferred_element_type=F32)
        for pr in range(M2_GW // LANES):
            ps = slice(g * M2_GW + pr * LANES, g * M2_GW + (pr + 1) * LANES)
            xp = x_dt[:, ps]
            y_pair = None
            for hh in range(2):
                h = (g * M2_GW + pr * LANES) // M2_HEADDIM + hh
                seg = jnp.exp(jnp.where(causal, cs[:, h:h + 1] - cs_t[h:h + 1, :], -jnp.inf))
                att = (cb * seg).astype(BF16)
                x_h = jnp.where((lane < M2_HEADDIM) == (hh == 0), xp, zero)
                part = jnp.dot(att, x_h, preferred_element_type=F32)
                y_pair = part if y_pair is None else y_pair + part
            y = y_pair + y_off[:, pr * LANES:(pr + 1) * LANES] + dexp_ref[:, ps] * xs[:, ps]
            gbuf[:, ps] = y * _silu(z_ref[0, :, ps].astype(F32))
    for g in range(M2_GROUPS):
        gs = slice(g * M2_GW, (g + 1) * M2_GW)
        gated = gbuf[:, gs]
        y_ref[0, :, gs] = (gated * lax.rsqrt(jnp.mean(gated * gated, axis=-1, keepdims=True) + RMS_EPS)
                           * ng_ref[:, gs]).astype(y_ref.dtype)


def _ssd(z, xbc, dt_raw, conv_w, conv_b, dt_bias, a_log, d, norm_g):
    b, s, _ = z.shape
    pad_row = lambda v: jnp.pad(v, (0, LANES - M2_HEADS)).reshape(1, LANES)
    expand = (jnp.arange(LANES)[:, None] == (jnp.arange(M2_INNER) // M2_HEADDIM)[None, :]).astype(BF16)
    tok = lambda w: pl.BlockSpec((1, M2_CHUNK, w), lambda i, j: (i, j, 0))
    fixed = lambda shape: pl.BlockSpec(shape, lambda i, j: (0, 0))
    return pl.pallas_call(
        _ssd_kernel, grid=(b, s // M2_CHUNK),
        in_specs=[tok(M2_INNER), tok(M2_CONV_DIM), tok(LANES),
                  fixed((M2_CONV, M2_CONV_DIM)), fixed((1, M2_CONV_DIM)), fixed((1, LANES)), fixed((1, LANES)),
                  fixed((1, M2_INNER)), fixed((1, M2_INNER)), fixed((LANES, M2_INNER))],
        out_specs=tok(M2_INNER), out_shape=jax.ShapeDtypeStruct((b, s, M2_INNER), BF16),
        scratch_shapes=[pltpu.VMEM((M2_CHUNK + 8, M2_CONV_DIM), F32),
                        pltpu.VMEM((M2_GROUPS, M2_STATE, M2_GW), F32),
                        pltpu.VMEM((M2_CHUNK, M2_INNER), F32)],
        compiler_params=_cparams(("parallel", "arbitrary")), name="ssd",
    )(z, xbc, dt_raw, conv_w, conv_b.reshape(1, -1), pad_row(dt_bias), pad_row(a_log),
      jnp.repeat(d, M2_HEADDIM).reshape(1, M2_INNER), norm_g.reshape(1, M2_INNER), expand)


def _cols(w, sizes):
    idx, out = 0, []
    for n in sizes:
        out.append(w[:, idx:idx + n])
        idx += n
    return out


def _even_layer(h, b, s, i, layer, p, mem_k, mem_v, rope_t, tm, tq):
    w_u, w_za, w_cq, w_ckv, w_kr, w_zb, w_qm, w_zm = _cols(
        p["ev_w_in"][i], (S5_WIDTH, S5_WIDTH, MLA_Q_LORA, MLA_KV_LORA, MLA_ROPE, MLA_WIDTH, MEM_WIDTH, MEM_WIDTH))
    w_kr = jnp.pad(w_kr, ((0, 0), (MLA_NOPE, LANES - MLA_QK)))
    weights = [w.astype(BF16) for w in (w_u, w_za, w_zb, w_qm, w_zm, w_cq, w_ckv, w_kr)]
    u, za, zb, qm, zm, cq, ckv, kr = _rms_proj(h, p["norm_g"][layer], weights, [BF16] * 5 + [F32] * 3, tm)

    n_steps = max(1, (s // S5_L - 1).bit_length())
    tables = _s5_tables(p["s5_a_re"][i], p["s5_a_im"][i], p["s5_log_dt"][i], p["s5_b_re"][i], p["s5_b_im"][i],
                        p["s5_c_re"][i], p["s5_c_im"][i], n_steps)
    y_s5 = _s5(u.reshape(b, s, S5_WIDTH), tables, p["s5_d"][i], gb=8).reshape(b * s, S5_WIDTH)

    qt, k, vt = _mla_prep(cq, ckv, kr, rope_t, p["mla_q_a_norm_g"][i], p["mla_w_uq"][i], p["mla_kv_a_norm_g"][i],
                          p["mla_w_ukv"][i], p["mla_q_norm_g"][i], p["mla_k_norm_g"][i], b, s, tq)
    y_b = _mla_attn(qt, k.reshape(b, s, MLA_HEADS * LANES), vt, zb.reshape(b, s, MLA_WIDTH),
                    tq).reshape(b * s, MLA_WIDTH)

    y_m = _mem_attn(qm.reshape(b, s, MEM_WIDTH), zm.reshape(b, s, MEM_WIDTH), mem_k, mem_v,
                    p["mem_q_norm_g"][layer], layer, tm).reshape(b * s, MEM_WIDTH)

    w_out = p["ev_w_out"][i].astype(BF16)
    ws = [w_out[:S5_WIDTH], w_out[S5_WIDTH:S5_WIDTH + MLA_WIDTH], w_out[S5_WIDTH + MLA_WIDTH:]]
    glu = (za, p["s5_glu_w"][i].astype(BF16), p["s5_glu_b"][i].reshape(1, S5_WIDTH))
    return _out_proj(h, [y_s5, y_b, y_m], ws, tm, glu)


def _odd_layer(h, b, s, i, layer, p, mem_k, mem_v, tm):
    w_z, w_xbc, w_dt, w_qm, w_zm = _cols(p["od_w_in"][i], (M2_INNER, M2_CONV_DIM, M2_HEADS, MEM_WIDTH, MEM_WIDTH))
    w_dt = jnp.pad(w_dt, ((0, 0), (0, LANES - M2_HEADS)))
    weights = [w.astype(BF16) for w in (w_z, w_xbc, w_qm, w_zm, w_dt)]
    z, xbc, qm, zm, dt_raw = _rms_proj(h, p["norm_g"][layer], weights, [BF16] * 4 + [F32], tm)
    y_c = _ssd(z.reshape(b, s, M2_INNER), xbc.reshape(b, s, M2_CONV_DIM), dt_raw.reshape(b, s, LANES),
               p["m2_conv_w"][i], p["m2_conv_b"][i], p["m2_dt_bias"][i], p["m2_a_log"][i], p["m2_d"][i],
               p["m2_norm_g"][i]).reshape(b * s, M2_INNER)
    y_m = _mem_attn(qm.reshape(b, s, MEM_WIDTH), zm.reshape(b, s, MEM_WIDTH), mem_k, mem_v,
                    p["mem_q_norm_g"][layer], layer, tm).reshape(b * s, MEM_WIDTH)
    w_out = p["od_w_out"][i].astype(BF16)
    return _out_proj(h, [y_c, y_m], [w_out[:M2_INNER], w_out[M2_INNER:]], tm)


def _token_tile(s):
    return 512 if s % 512 == 0 else s


def kernel(x, mem, positions, norm_g, mem_norm_g, mem_w_kv, mem_q_norm_g, mem_k_norm_g, ev_w_in, ev_w_out, s5_a_re, s5_a_im, s5_log_dt, s5_b_re, s5_b_im, s5_c_re, s5_c_im, s5_d, s5_glu_w, s5_glu_b, mla_q_a_norm_g, mla_w_uq, mla_kv_a_norm_g, mla_w_ukv, mla_q_norm_g, mla_k_norm_g, od_w_in, od_w_out, m2_conv_w, m2_conv_b, m2_dt_bias, m2_a_log, m2_d, m2_norm_g):
    p = dict(norm_g=norm_g, mem_q_norm_g=mem_q_norm_g, ev_w_in=ev_w_in, ev_w_out=ev_w_out,
             s5_a_re=s5_a_re, s5_a_im=s5_a_im, s5_log_dt=s5_log_dt, s5_b_re=s5_b_re, s5_b_im=s5_b_im,
             s5_c_re=s5_c_re, s5_c_im=s5_c_im, s5_d=s5_d, s5_glu_w=s5_glu_w, s5_glu_b=s5_glu_b,
             mla_q_a_norm_g=mla_q_a_norm_g, mla_w_uq=mla_w_uq, mla_kv_a_norm_g=mla_kv_a_norm_g,
             mla_w_ukv=mla_w_ukv, mla_q_norm_g=mla_q_norm_g, mla_k_norm_g=mla_k_norm_g,
             od_w_in=od_w_in, od_w_out=od_w_out, m2_conv_w=m2_conv_w, m2_conv_b=m2_conv_b,
             m2_dt_bias=m2_dt_bias, m2_a_log=m2_a_log, m2_d=m2_d, m2_norm_g=m2_norm_g)
    b, s, d = x.shape
    tm = _token_tile(s)
    tq = 512 if s % 512 == 0 else s
    mem_k, mem_v = _mem_kv(mem, mem_norm_g, mem_w_kv, mem_k_norm_g)
    rope_t = _rope_tables(positions)
    h = x.reshape(b * s, d)
    for layer in range(DEPTH):
        if layer % 2 == 0:
            h = _even_layer(h, b, s, layer // 2, layer, p, mem_k, mem_v, rope_t, tm, tq)
        else:
            h = _odd_layer(h, b, s, layer // 2, layer, p, mem_k, mem_v, tm)
    return h.reshape(b, s, d)
```

```python
import functools
import math

import jax
import jax.numpy as jnp
from jax import lax
from jax.experimental import pallas as pl
from jax.experimental.pallas import tpu as pltpu

F32 = jnp.float32
BF16 = jnp.bfloat16

D_MODEL = 1024
DEPTH = 4
CHUNK = 64
N_MEM = 256
RMS_EPS = 1e-6

S5_WIDTH = 512
S5_GROUP = 16
S5_GROUPS = S5_WIDTH // S5_GROUP
S5_STATE = 64
S5_L = 16
S5_FLAT = S5_L * S5_GROUP

MLA_HEADS = 8
MLA_NOPE = 64
MLA_ROPE = 32
MLA_QK = MLA_NOPE + MLA_ROPE
MLA_V = 64
MLA_WIDTH = MLA_HEADS * MLA_V
MLA_Q_LORA = 256
MLA_KV_LORA = 128
ROPE_BASE = 10000.0
ROPE_HALF = MLA_ROPE // 2
LANES = 128

M2_INNER = 2 * D_MODEL
M2_HEADDIM = 64
M2_HEADS = M2_INNER // M2_HEADDIM
M2_GROUPS = 4
M2_STATE = 128
M2_CONV = 4
M2_CHUNK = 128
M2_CONV_DIM = M2_INNER + 2 * M2_GROUPS * M2_STATE
M2_GW = M2_INNER // M2_GROUPS

MEM_HEADS = 4
MEM_HD = 128
MEM_WIDTH = MEM_HEADS * MEM_HD

VMEM_LIMIT = 56 * 1024 * 1024


def _cparams(sem):
    return pltpu.CompilerParams(dimension_semantics=sem, vmem_limit_bytes=VMEM_LIMIT)


def _silu(z):
    h = 0.5 * z
    return h + h * jnp.tanh(h)


def _rms_proj_kernel(h_ref, g_ref, *refs, n_out, col_chunk):
    w_refs, o_refs = refs[:n_out], refs[n_out:]
    x = h_ref[...]
    xn = (x * lax.rsqrt(jnp.mean(x * x, axis=-1, keepdims=True) + RMS_EPS) * g_ref[...]).astype(BF16)
    for w_ref, o_ref in zip(w_refs, o_refs):
        n = w_ref.shape[1]
        for c0 in range(0, n, col_chunk):
            c1 = min(n, c0 + col_chunk)
            o_ref[:, c0:c1] = jnp.dot(xn, w_ref[:, c0:c1], preferred_element_type=F32).astype(o_ref.dtype)


def _rms_proj(h, g, weights, out_dtypes, tm):
    t, d = h.shape
    in_specs = [pl.BlockSpec((tm, d), lambda i: (i, 0)), pl.BlockSpec((1, d), lambda i: (0, 0))]
    in_specs += [pl.BlockSpec(w.shape, lambda i: (0, 0)) for w in weights]
    out_specs = [pl.BlockSpec((tm, w.shape[1]), lambda i: (i, 0)) for w in weights]
    out_shape = [jax.ShapeDtypeStruct((t, w.shape[1]), dt) for w, dt in zip(weights, out_dtypes)]
    return pl.pallas_call(
        functools.partial(_rms_proj_kernel, n_out=len(weights), col_chunk=512),
        grid=(t // tm,), in_specs=in_specs, out_specs=out_specs, out_shape=out_shape,
        compiler_params=_cparams(("parallel",)), name="rms_proj",
    )(h, g.reshape(1, d), *weights)


def _out_proj_kernel(h_ref, *refs, n_in, glu):
    x_refs, w_refs = refs[:n_in], refs[n_in:2 * n_in]
    o_ref = refs[-1]
    acc = h_ref[...]
    for i, (x_ref, w_ref) in enumerate(zip(x_refs, w_refs)):
        x = x_ref[...]
        if glu and i == 0:
            za_ref, gw_ref, gb_ref = refs[2 * n_in:2 * n_in + 3]
            gate = jnp.dot(x, gw_ref[...], preferred_element_type=F32) + gb_ref[...]
            x = (x.astype(F32) * jax.nn.sigmoid(gate) * _silu(za_ref[...].astype(F32))).astype(BF16)
        acc = acc + jnp.dot(x, w_ref[...], preferred_element_type=F32)
    o_ref[...] = acc


def _out_proj(h, xs, ws, tm, glu_args=None):
    t, d = h.shape
    row = lambda i: (i, 0)
    fixed = lambda i: (0, 0)
    in_specs = [pl.BlockSpec((tm, d), row)]
    in_specs += [pl.BlockSpec((tm, x.shape[1]), row) for x in xs]
    in_specs += [pl.BlockSpec(w.shape, fixed) for w in ws]
    args = [h, *xs, *ws]
    if glu_args is not None:
        za, gw, gb = glu_args
        in_specs += [pl.BlockSpec((tm, za.shape[1]), row), pl.BlockSpec(gw.shape, fixed),
                     pl.BlockSpec(gb.shape, fixed)]
        args += [za, gw, gb]
    return pl.pallas_call(
        functools.partial(_out_proj_kernel, n_in=len(xs), glu=glu_args is not None),
        grid=(t // tm,), in_specs=in_specs, out_specs=pl.BlockSpec((tm, d), row),
        out_shape=jax.ShapeDtypeStruct((t, d), F32),
        compiler_params=_cparams(("parallel",)), name="out_proj",
    )(*args)


def _mem_kv_kernel(mem_ref, g_ref, w_ref, kg_ref, k_ref, v_ref):
    x = mem_ref[0]
    xn = (x * lax.rsqrt(jnp.mean(x * x, axis=-1, keepdims=True) + RMS_EPS) * g_ref[0]).astype(BF16)
    kv = jnp.dot(xn, w_ref[0], preferred_element_type=F32)
    for h in range(MEM_HEADS):
        kh = kv[:, h * MEM_HD:(h + 1) * MEM_HD]
        kn = kh * lax.rsqrt(jnp.mean(kh * kh, axis=-1, keepdims=True) + RMS_EPS) * kg_ref[0]
        k_ref[0, 0, :, h * MEM_HD:(h + 1) * MEM_HD] = kn.astype(BF16)
    v_ref[0, 0] = kv[:, MEM_WIDTH:].astype(BF16)


def _mem_kv(mem, mem_norm_g, w_kv, k_norm_g):
    b = mem.shape[0]
    out = jax.ShapeDtypeStruct((DEPTH, b, N_MEM, MEM_WIDTH), BF16)
    return pl.pallas_call(
        _mem_kv_kernel, grid=(DEPTH, b),
        in_specs=[pl.BlockSpec((1, N_MEM, D_MODEL), lambda l, i: (i, 0, 0)),
                  pl.BlockSpec((1, 1, D_MODEL), lambda l, i: (l, 0, 0)),
                  pl.BlockSpec((1, D_MODEL, 2 * MEM_WIDTH), lambda l, i: (l, 0, 0)),
                  pl.BlockSpec((1, 1, MEM_HD), lambda l, i: (l, 0, 0))],
        out_specs=[pl.BlockSpec((1, 1, N_MEM, MEM_WIDTH), lambda l, i: (l, i, 0, 0))] * 2,
        out_shape=[out, out], compiler_params=_cparams(("arbitrary", "arbitrary")), name="mem_kv",
    )(mem, mem_norm_g.reshape(DEPTH, 1, D_MODEL), w_kv.astype(BF16), k_norm_g.reshape(DEPTH, 1, MEM_HD))


def _mem_attn_kernel(q_ref, z_ref, k_ref, v_ref, qg_ref, o_ref):
    scale = 1.0 / math.sqrt(MEM_HD)
    for h in range(MEM_HEADS):
        sl = slice(h * MEM_HD, (h + 1) * MEM_HD)
        q = q_ref[0, :, sl].astype(F32)
        qn = (q * lax.rsqrt(jnp.mean(q * q, axis=-1, keepdims=True) + RMS_EPS) * (qg_ref[...] * scale)).astype(BF16)
        s = lax.dot_general(qn, k_ref[0, 0, :, sl], (((1,), (1,)), ((), ())), preferred_element_type=F32)
        p = jnp.exp(s - jnp.max(s, axis=-1, keepdims=True))
        l = jnp.sum(p, axis=-1, keepdims=True)
        o = jnp.dot(p.astype(BF16), v_ref[0, 0, :, sl], preferred_element_type=F32) / l
        o_ref[0, :, sl] = (o * _silu(z_ref[0, :, sl].astype(F32))).astype(BF16)


def _mem_attn(q, z, k_all, v_all, qg, layer, tq):
    b, s, _ = q.shape
    tok = pl.BlockSpec((1, tq, MEM_WIDTH), lambda i, j: (i, j, 0))
    bank = pl.BlockSpec((1, 1, N_MEM, MEM_WIDTH), lambda i, j: (layer, i, 0, 0))
    return pl.pallas_call(
        _mem_attn_kernel, grid=(b, s // tq),
        in_specs=[tok, tok, bank, bank, pl.BlockSpec((1, MEM_HD), lambda i, j: (0, 0))],
        out_specs=tok, out_shape=jax.ShapeDtypeStruct((b, s, MEM_WIDTH), BF16),
        compiler_params=_cparams(("parallel", "parallel")), name="mem_attn",
    )(q, z, k_all, v_all, qg.reshape(1, MEM_HD))


def _rope_table_kernel(pos_ref, inv_ref, cos_ref, sin_ref):
    ang = pos_ref[0].astype(F32) * inv_ref[...]
    cos_ref[0] = jnp.cos(ang)
    sin_ref[0] = jnp.sin(ang)


def _rope_tables(positions):
    b, s = positions.shape
    inv = ROPE_BASE ** (-jnp.arange(ROPE_HALF, dtype=F32) / ROPE_HALF)
    out = jax.ShapeDtypeStruct((b, ROPE_HALF, s), F32)
    cos, sin = pl.pallas_call(
        _rope_table_kernel, grid=(b,),
        in_specs=[pl.BlockSpec((1, 1, s), lambda i: (i, 0, 0)), pl.BlockSpec((ROPE_HALF, 1), lambda i: (0, 0))],
        out_specs=[pl.BlockSpec((1, ROPE_HALF, s), lambda i: (i, 0, 0))] * 2,
        out_shape=[out, out], compiler_params=_cparams(("parallel",)), name="rope_tables",
    )(positions.reshape(b, 1, s), inv.reshape(ROPE_HALF, 1))
    tail = LANES - MLA_QK
    cos_q = jnp.concatenate([jnp.ones((b, MLA_NOPE, s), F32), cos, cos, jnp.ones((b, tail, s), F32)], 1)
    sin_q = jnp.concatenate([jnp.zeros((b, MLA_NOPE, s), F32), -sin, sin, jnp.zeros((b, tail, s), F32)], 1)
    cos_k = jnp.swapaxes(cos_q, 1, 2).reshape(b * s, LANES)
    sin_k = jnp.swapaxes(sin_q, 1, 2).reshape(b * s, LANES)
    return cos_q, sin_q, cos_k, sin_k


def _head_norm_rope(x, gain, cos, sin, lane):
    xn = x * lax.rsqrt(jnp.sum(x * x, axis=-1, keepdims=True) * (1.0 / MLA_QK) + RMS_EPS) * gain
    partner = jnp.where(lane < MLA_NOPE + ROPE_HALF,
                        pltpu.roll(xn, LANES - ROPE_HALF, 1), pltpu.roll(xn, ROPE_HALF, 1))
    return xn * cos + partner * sin


def _mla_prep_kernel(cq_ref, ckv_ref, kr_ref, cosq_ref, sinq_ref, cosk_ref, sink_ref, gqa_ref, wqt_ref, gkva_ref,
                     wk_ref, wvt_ref, gq_ref, gk_ref, qt_ref, k_ref, vt_ref):
    cq = cq_ref[...]
    cqn = cq * lax.rsqrt(jnp.mean(cq * cq, axis=-1, keepdims=True) + RMS_EPS) * gqa_ref[...]
    ckv = ckv_ref[...]
    ckvn = ckv * lax.rsqrt(jnp.mean(ckv * ckv, axis=-1, keepdims=True) + RMS_EPS) * gkva_ref[...]
    cqn_t = cqn.T.astype(BF16)
    ckvn_t = ckvn.T.astype(BF16)
    ckvn = ckvn.astype(BF16)
    kr = kr_ref[...]
    cosq, sinq, cosk, sink = cosq_ref[0], sinq_ref[0], cosk_ref[...], sink_ref[...]
    lane = lax.broadcasted_iota(jnp.int32, cosk.shape, 1)
    row = lax.broadcasted_iota(jnp.int32, cosq.shape, 0)
    qscale = math.log2(math.e) / math.sqrt(MLA_QK)
    for h in range(MLA_HEADS):
        sl = slice(h * LANES, (h + 1) * LANES)
        qt = jnp.dot(wqt_ref[sl, :], cqn_t, preferred_element_type=F32)
        qn = qt * lax.rsqrt(jnp.sum(qt * qt, axis=0, keepdims=True) * (1.0 / MLA_QK) + RMS_EPS) * gq_ref[...]
        partner = jnp.where(row < MLA_NOPE + ROPE_HALF,
                            pltpu.roll(qn, LANES - ROPE_HALF, 0), pltpu.roll(qn, ROPE_HALF, 0))
        qt_ref[0, 0, sl, :] = ((qn * cosq + partner * sinq) * qscale).astype(BF16)
        kh = jnp.dot(ckvn, wk_ref[:, sl], preferred_element_type=F32) + kr
        k_ref[:, sl] = _head_norm_rope(kh, gk_ref[...], cosk, sink, lane).astype(BF16)
        vs = slice(h * MLA_V, (h + 1) * MLA_V)
        vt_ref[0, 0, vs, :] = jnp.dot(wvt_ref[vs, :], ckvn_t, preferred_element_type=F32).astype(BF16)


def _pad_heads(w, n_heads, width, offset=0):
    k = w.shape[0]
    w = w.reshape(k, n_heads, width)
    w = jnp.pad(w, ((0, 0), (0, 0), (offset, LANES - width - offset)))
    return w.reshape(k, n_heads * LANES)


def _mla_prep(cq, ckv, kr, rope_t, gqa, w_uq, gkva, w_ukv, gq, gk, b, s, tm):
    cos_q, sin_q, cos_k, sin_k = rope_t
    nt = s // tm
    wqt = _pad_heads(w_uq, MLA_HEADS, MLA_QK).T.astype(BF16)
    w_ukv = w_ukv.reshape(MLA_KV_LORA, MLA_HEADS, MLA_NOPE + MLA_V)
    wk = _pad_heads(w_ukv[:, :, :MLA_NOPE].reshape(MLA_KV_LORA, -1), MLA_HEADS, MLA_NOPE).astype(BF16)
    wvt = w_ukv[:, :, MLA_NOPE:].reshape(MLA_KV_LORA, MLA_WIDTH).T.astype(BF16)
    pad_gain = lambda g: jnp.pad(g, (0, LANES - MLA_QK))
    row = lambda i, j: (i * nt + j, 0)
    fixed = lambda i, j: (0, 0)
    hw = MLA_HEADS * LANES
    return pl.pallas_call(
        _mla_prep_kernel, grid=(b, nt),
        in_specs=[pl.BlockSpec((tm, MLA_Q_LORA), row), pl.BlockSpec((tm, MLA_KV_LORA), row),
                  pl.BlockSpec((tm, LANES), row),
                  pl.BlockSpec((1, LANES, tm), lambda i, j: (i, 0, j)), pl.BlockSpec((1, LANES, tm), lambda i, j: (i, 0, j)),
                  pl.BlockSpec((tm, LANES), row), pl.BlockSpec((tm, LANES), row),
                  pl.BlockSpec((1, MLA_Q_LORA), fixed), pl.BlockSpec((hw, MLA_Q_LORA), fixed),
                  pl.BlockSpec((1, MLA_KV_LORA), fixed), pl.BlockSpec((MLA_KV_LORA, hw), fixed),
                  pl.BlockSpec((MLA_WIDTH, MLA_KV_LORA), fixed),
                  pl.BlockSpec((LANES, 1), fixed), pl.BlockSpec((1, LANES), fixed)],
        out_specs=[pl.BlockSpec((1, 1, hw, tm), lambda i, j: (i, j, 0, 0)), pl.BlockSpec((tm, hw), row),
                   pl.BlockSpec((1, 1, MLA_WIDTH, tm), lambda i, j: (i, j, 0, 0))],
        out_shape=[jax.ShapeDtypeStruct((b, nt, hw, tm), BF16), jax.ShapeDtypeStruct((b * s, hw), BF16),
                   jax.ShapeDtypeStruct((b, nt, MLA_WIDTH, tm), BF16)],
        compiler_params=_cparams(("parallel", "parallel")), name="mla_prep",
    )(cq, ckv, kr, cos_q, sin_q, cos_k, sin_k, gqa.reshape(1, -1), wqt, gkva.reshape(1, -1), wk, wvt,
      pad_gain(gq).reshape(LANES, 1), pad_gain(gk).reshape(1, LANES))


def _mla_attn_kernel(q_ref, k_ref, v_ref, z_ref, o_ref, m_sc, l_sc, acc_sc, *, tq):
    qi = pl.program_id(2)
    m_sc[...] = jnp.full(m_sc.shape, -jnp.inf, F32)
    l_sc[...] = jnp.zeros(l_sc.shape, F32)
    acc_sc[...] = jnp.zeros(acc_sc.shape, F32)

    def block(ki, masked):
        start = pl.multiple_of(ki * tq, tq)
        for h in range(2):
            s = jnp.dot(k_ref[0, pl.ds(start, tq), h * LANES:(h + 1) * LANES], q_ref[0, 0, h * LANES:(h + 1) * LANES, :],
                        preferred_element_type=F32)
            if masked:
                kc = lax.broadcasted_iota(jnp.int32, s.shape, 0) // CHUNK
                qc = lax.broadcasted_iota(jnp.int32, s.shape, 1) // CHUNK
                s = jnp.where(kc <= qc, s, -jnp.inf)
            m_prev = m_sc[h]
            m_new = jnp.maximum(m_prev, jnp.max(s, axis=0, keepdims=True))
            alpha = jnp.exp2(m_prev - m_new)
            p = jnp.exp2(s - m_new)
            l_sc[h] = alpha * l_sc[h] + jnp.sum(p, axis=0, keepdims=True)
            acc_sc[h] = alpha * acc_sc[h] + jnp.dot(v_ref[0, ki, h * MLA_V:(h + 1) * MLA_V, :], p.astype(BF16),
                                                    preferred_element_type=F32)
            m_sc[h] = m_new

    def body(ki, carry):
        block(ki, masked=False)
        return carry

    lax.fori_loop(0, qi, body, 0)
    block(qi, masked=True)
    o_t = jnp.concatenate([acc_sc[0] / l_sc[0], acc_sc[1] / l_sc[1]], axis=0)
    o_ref[0] = (o_t.T * _silu(z_ref[0].astype(F32))).astype(BF16)


def _mla_attn(qt, k, vt, z, tq):
    b, s, _ = k.shape
    nt = s // tq
    pairs = MLA_HEADS // 2
    return pl.pallas_call(
        functools.partial(_mla_attn_kernel, tq=tq), grid=(b, pairs, nt),
        in_specs=[pl.BlockSpec((1, 1, 2 * LANES, tq), lambda i, p, j: (i, j, p, 0)),
                  pl.BlockSpec((1, s, 2 * LANES), lambda i, p, j: (i, 0, p)),
                  pl.BlockSpec((1, nt, 2 * MLA_V, tq), lambda i, p, j: (i, 0, p, 0)),
                  pl.BlockSpec((1, tq, LANES), lambda i, p, j: (i, j, p))],
        out_specs=pl.BlockSpec((1, tq, LANES), lambda i, p, j: (i, j, p)),
        out_shape=jax.ShapeDtypeStruct((b, s, MLA_WIDTH), BF16),
        scratch_shapes=[pltpu.VMEM((2, 1, tq), F32), pltpu.VMEM((2, 1, tq), F32),
                        pltpu.VMEM((2, MLA_V, tq), F32)],
        compiler_params=_cparams(("parallel", "parallel", "arbitrary")), name="mla_attn",
    )(qt, k, vt, z)


def _s5_tables(a_re, a_im, log_dt, b_re, b_im, c_re, c_im, n_steps):
    hp = lax.Precision.HIGHEST
    dt = jnp.exp(log_dt)[:, None]
    mag = jnp.exp(a_re * dt)
    ab_re, ab_im = mag * jnp.cos(a_im * dt), mag * jnp.sin(a_im * dt)
    den = a_re * a_re + a_im * a_im
    n_re, n_im = ab_re - 1.0, ab_im
    f_re = (n_re * a_re + n_im * a_im) / den
    f_im = (n_im * a_re - n_re * a_im) / den
    bb_re = f_re[..., None] * b_re - f_im[..., None] * b_im
    bb_im = f_re[..., None] * b_im + f_im[..., None] * b_re

    def lam_pow(k):
        k = k.astype(F32)[:, None, None]
        m = jnp.exp(k * a_re * dt)
        return m * jnp.cos(k * a_im * dt), m * jnp.sin(k * a_im * dt)

    L, C, P, G = S5_L, S5_GROUP, S5_STATE, S5_GROUPS
    lr, li = lam_pow(jnp.arange(L + 1))
    w_re = lr[..., None] * bb_re[None] - li[..., None] * bb_im[None]
    w_im = lr[..., None] * bb_im[None] + li[..., None] * bb_re[None]
    kk = (jnp.einsum("gdp,kgpc->kgdc", c_re, w_re[:L], precision=hp)
          - jnp.einsum("gdp,kgpc->kgdc", c_im, w_im[:L], precision=hp))
    lag = jnp.arange(L)[None, :] - jnp.arange(L)[:, None]
    kfull = jnp.where((lag >= 0)[:, :, None, None, None], kk[jnp.clip(lag, 0, L - 1)], 0.0)
    mt = jnp.transpose(kfull, (2, 0, 4, 1, 3)).reshape(G, L * C, L * C)
    e_re = jnp.transpose(w_re[:L][::-1], (1, 0, 3, 2)).reshape(G, L * C, P)
    e_im = jnp.transpose(w_im[:L][::-1], (1, 0, 3, 2)).reshape(G, L * C, P)
    w1 = jnp.concatenate([mt, e_re, e_im, -e_im, e_re], axis=-1)
    g_re = c_re[None] * lr[1:, :, None, :] - c_im[None] * li[1:, :, None, :]
    g_im = c_re[None] * li[1:, :, None, :] + c_im[None] * lr[1:, :, None, :]
    f_mat = jnp.concatenate([jnp.transpose(g_re, (1, 3, 0, 2)).reshape(G, P, L * C),
                             -jnp.transpose(g_im, (1, 3, 0, 2)).reshape(G, P, L * C)], axis=1)
    sr, si = lam_pow(L * (2 ** jnp.arange(n_steps)))
    la = jnp.transpose(jnp.concatenate([sr, sr], -1), (1, 0, 2))[:, :, None, :]
    lb = jnp.transpose(jnp.concatenate([si, si], -1), (1, 0, 2))[:, :, None, :]
    return w1.astype(BF16), f_mat.astype(BF16), la, lb


def _s5_kernel(u_ref, w1_ref, f_ref, la_ref, lb_ref, d_ref, y_ref, *, gb, n_steps):
    nc = u_ref.shape[2]
    row = lax.broadcasted_iota(jnp.int32, (nc, 2 * S5_STATE), 0)

    def shift(t, d):
        return jnp.where(row >= d, pltpu.roll(t, d, 0), 0.0)

    for g in range(gb):
        u = u_ref[0, g]
        r = jnp.dot(u, w1_ref[g], preferred_element_type=F32)
        y = r[:, :S5_FLAT]
        x = r[:, S5_FLAT:S5_FLAT + 2 * S5_STATE]
        xs = r[:, S5_FLAT + 2 * S5_STATE:]
        for s in range(n_steps):
            a, b = la_ref[g, s], lb_ref[g, s]
            tx, txs = a * x + b * xs, a * xs - b * x
            x, xs = x + shift(tx, 1 << s), xs + shift(txs, 1 << s)
        h_in = shift(x, 1).astype(BF16)
        y = y + jnp.dot(h_in, f_ref[g], preferred_element_type=F32) + d_ref[g] * u.astype(F32)
        y_ref[0, g] = jax.nn.gelu(y, approximate=True).astype(BF16)


def _s5(u, tables, d, gb):
    w1, f_mat, la, lb = tables
    b, s, _ = u.shape
    nc = s // S5_L
    n_steps = la.shape[1]
    G = S5_GROUPS
    uf = u.reshape(b, nc, S5_L, G, S5_GROUP).transpose(0, 3, 1, 2, 4).reshape(b, G, nc, S5_FLAT)
    d_flat = jnp.tile(d.reshape(G, 1, S5_GROUP), (1, 1, S5_L))
    grp = lambda i, j: (j, 0, 0)
    y = pl.pallas_call(
        functools.partial(_s5_kernel, gb=gb, n_steps=n_steps), grid=(b, G // gb),
        in_specs=[pl.BlockSpec((1, gb, nc, S5_FLAT), lambda i, j: (i, j, 0, 0)),
                  pl.BlockSpec((gb, S5_FLAT, 2 * S5_FLAT), grp),
                  pl.BlockSpec((gb, 2 * S5_STATE, S5_FLAT), grp),
                  pl.BlockSpec((gb, n_steps, 1, 2 * S5_STATE), lambda i, j: (j, 0, 0, 0)),
                  pl.BlockSpec((gb, n_steps, 1, 2 * S5_STATE), lambda i, j: (j, 0, 0, 0)),
                  pl.BlockSpec((gb, 1, S5_FLAT), grp)],
        out_specs=pl.BlockSpec((1, gb, nc, S5_FLAT), lambda i, j: (i, j, 0, 0)),
        out_shape=jax.ShapeDtypeStruct((b, G, nc, S5_FLAT), BF16),
        compiler_params=_cparams(("parallel", "parallel")), name="s5",
    )(uf, w1, f_mat, la, lb, d_flat)
    return y.reshape(b, G, nc, S5_L, S5_GROUP).transpose(0, 2, 3, 1, 4).reshape(b, s, S5_WIDTH)


def _split_dot(x, e):
    hi = x.astype(BF16)
    lo = (x - hi.astype(F32)).astype(BF16)
    return jnp.dot(hi, e, preferred_element_type=F32) + jnp.dot(lo, e, preferred_element_type=F32)


def _ssd_kernel(z_ref, xbc_ref, dt_ref, cw_ref, cb_ref, dtb_ref, alog_ref, dexp_ref, ng_ref, e_ref,
                y_ref, xpad, state, gbuf):
    L, PAD = M2_CHUNK, 8

    @pl.when(pl.program_id(1) == 0)
    def _():
        xpad[0:PAD, :] = jnp.zeros((PAD, M2_CONV_DIM), F32)
        state[...] = jnp.zeros_like(state)

    xpad[PAD:PAD + L, :] = xbc_ref[0].astype(F32)
    acc = cb_ref[...] + cw_ref[0:1, :] * xpad[PAD - 3:PAD - 3 + L, :]
    for k in range(1, M2_CONV):
        acc = acc + cw_ref[k:k + 1, :] * xpad[PAD - 3 + k:PAD - 3 + k + L, :]
    xpad[0:PAD, :] = xpad[L:L + PAD, :]
    xc = _silu(acc)
    xs = xc[:, :M2_INNER]
    nbc = M2_GROUPS * M2_STATE
    bm = xc[:, M2_INNER:M2_INNER + nbc]
    cm = xc[:, M2_INNER + nbc:]

    dt = jax.nn.softplus(dt_ref[0] + dtb_ref[...])
    da = dt * (-jnp.exp(alog_ref[...]))
    ti = lax.broadcasted_iota(jnp.int32, (L, L), 0)
    si = lax.broadcasted_iota(jnp.int32, (L, L), 1)
    causal = si <= ti
    tril =jnp.where(causal, 1.0, 0.0).astype(BF16)
    hi = da.astype(BF16)
    r1 = da - hi.astype(F32)
    mid = r1.astype(BF16)
    lo = (r1 - mid.astype(F32)).astype(BF16)
    cs = (jnp.dot(tril, hi, preferred_element_type=F32) + jnp.dot(tril, mid, preferred_element_type=F32)
          + jnp.dot(tril, lo, preferred_element_type=F32))
    cs2 = cs * math.log2(math.e)
    cs2_t = cs2.T
    cs_end = cs[L - 1:L, :]
    stack = jnp.concatenate([dt, dt * jnp.exp(cs_end - cs), jnp.exp(cs)], axis=0).astype(BF16)
    fac = jnp.dot(stack, e_ref[...], preferred_element_type=F32)
    dt_e, dw_e, ecs_e = fac[:L], fac[L:2 * L], fac[2 * L:]
    dend_e = _split_dot(jnp.broadcast_to(jnp.exp(cs_end), (PAD, LANES)), e_ref[...])[0:1]
    x_dt = (xs * dt_e).astype(BF16)
    x_w = (xs * dw_e).astype(BF16)

    lane = lax.broadcasted_iota(jnp.int32, (L, LANES), 1)
    zero = jnp.zeros((L, LANES), BF16)
    for g in range(M2_GROUPS):
        gs = slice(g * M2_GW, (g + 1) * M2_GW)
        b_g = bm[:, g * M2_STATE:(g + 1) * M2_STATE]
        c_g = cm[:, g * M2_STATE:(g + 1) * M2_STATE].astype(BF16)
        cb = lax.dot_general(c_g, b_g.astype(BF16), (((1,), (1,)), ((), ())), preferred_element_type=F32)
        s_old = state[g]
        y_off = jnp.dot(c_g, s_old.astype(BF16), preferred_element_type=F32) * ecs_e[:, gs]
        state[g] = s_old * dend_e[:, gs] + jnp.dot(b_g.T.astype(BF16), x_w[:, gs], preferred_element_type=F32)
        for pr in range(M2_GW // LANES):
            ps = slice(g * M2_GW + pr * LANES, g * M2_GW + (pr + 1) * LANES)
            xp = x_dt[:, ps]
            y_pair = None
            for hh in range(2):
                h = (g * M2_GW + pr * LANES) // M2_HEADDIM + hh
                seg = jnp.exp2(jnp.where(causal, cs2[:, h:h + 1] - cs2_t[h:h + 1, :], -jnp.inf))
                att = (cb * seg).astype(BF16)
                x_h = jnp.where((lane < M2_HEADDIM) == (hh == 0), xp, zero)
                part = jnp.dot(att, x_h, preferred_element_type=F32)
                y_pair = part if y_pair is None else y_pair + part
            y = y_pair + y_off[:, pr * LANES:(pr + 1) * LANES] + dexp_ref[:, ps] * xs[:, ps]
            gbuf[:, ps] = y * _silu(z_ref[0, :, ps].astype(F32))
    for g in range(M2_GROUPS):
        gs = slice(g * M2_GW, (g + 1) * M2_GW)
        gated = gbuf[:, gs]
        y_ref[0, :, gs] = (gated * lax.rsqrt(jnp.mean(gated * gated, axis=-1, keepdims=True) + RMS_EPS)
                           * ng_ref[:, gs]).astype(y_ref.dtype)


def _ssd(z, xbc, dt_raw, conv_w, conv_b, dt_bias, a_log, d, norm_g):
    b, s, _ = z.shape
    pad_row = lambda v: jnp.pad(v, (0, LANES - M2_HEADS)).reshape(1, LANES)
    expand = (jnp.arange(LANES)[:, None] == (jnp.arange(M2_INNER) // M2_HEADDIM)[None, :]).astype(BF16)
    tok = lambda w: pl.BlockSpec((1, M2_CHUNK, w), lambda i, j: (i, j, 0))
    fixed = lambda shape: pl.BlockSpec(shape, lambda i, j: (0, 0))
    return pl.pallas_call(
        _ssd_kernel, grid=(b, s // M2_CHUNK),
        in_specs=[tok(M2_INNER), tok(M2_CONV_DIM), tok(LANES),
                  fixed((M2_CONV, M2_CONV_DIM)), fixed((1, M2_CONV_DIM)), fixed((1, LANES)), fixed((1, LANES)),
                  fixed((1, M2_INNER)), fixed((1, M2_INNER)), fixed((LANES, M2_INNER))],
        out_specs=tok(M2_INNER), out_shape=jax.ShapeDtypeStruct((b, s, M2_INNER), BF16),
        scratch_shapes=[pltpu.VMEM((M2_CHUNK + 8, M2_CONV_DIM), F32),
                        pltpu.VMEM((M2_GROUPS, M2_STATE, M2_GW), F32),
                        pltpu.VMEM((M2_CHUNK, M2_INNER), F32)],
        compiler_params=_cparams(("parallel", "arbitrary")), name="ssd",
    )(z, xbc, dt_raw, conv_w, conv_b.reshape(1, -1), pad_row(dt_bias), pad_row(a_log),
      jnp.repeat(d, M2_HEADDIM).reshape(1, M2_INNER), norm_g.reshape(1, M2_INNER), expand)


def _cols(w, sizes):
    idx, out = 0, []
    for n in sizes:
        out.append(w[:, idx:idx + n])
        idx += n
    return out


def _even_layer(h, b, s, i, layer, p, mem_k, mem_v, rope_t, tm, tq):
    w_u, w_za, w_cq, w_ckv, w_kr, w_zb, w_qm, w_zm = _cols(
        p["ev_w_in"][i], (S5_WIDTH, S5_WIDTH, MLA_Q_LORA, MLA_KV_LORA, MLA_ROPE, MLA_WIDTH, MEM_WIDTH, MEM_WIDTH))
    w_kr = jnp.pad(w_kr, ((0, 0), (MLA_NOPE, LANES - MLA_QK)))
    weights = [w.astype(BF16) for w in (w_u, w_za, w_zb, w_qm, w_zm, w_cq, w_ckv, w_kr)]
    u, za, zb, qm, zm, cq, ckv, kr = _rms_proj(h, p["norm_g"][layer], weights, [BF16] * 5 + [F32] * 3, tm)

    n_steps = max(1, (s // S5_L - 1).bit_length())
    tables = _s5_tables(p["s5_a_re"][i], p["s5_a_im"][i], p["s5_log_dt"][i], p["s5_b_re"][i], p["s5_b_im"][i],
                        p["s5_c_re"][i], p["s5_c_im"][i], n_steps)
    y_s5 = _s5(u.reshape(b, s, S5_WIDTH), tables, p["s5_d"][i], gb=8).reshape(b * s, S5_WIDTH)

    qt, k, vt = _mla_prep(cq, ckv, kr, rope_t, p["mla_q_a_norm_g"][i], p["mla_w_uq"][i], p["mla_kv_a_norm_g"][i],
                          p["mla_w_ukv"][i], p["mla_q_norm_g"][i], p["mla_k_norm_g"][i], b, s, tq)
    y_b = _mla_attn(qt, k.reshape(b, s, MLA_HEADS * LANES), vt, zb.reshape(b, s, MLA_WIDTH),
                    tq).reshape(b * s, MLA_WIDTH)

    y_m = _mem_attn(qm.reshape(b, s, MEM_WIDTH), zm.reshape(b, s, MEM_WIDTH), mem_k, mem_v,
                    p["mem_q_norm_g"][layer], layer, tm).reshape(b * s, MEM_WIDTH)

    w_out = p["ev_w_out"][i].astype(BF16)
    ws = [w_out[:S5_WIDTH], w_out[S5_WIDTH:S5_WIDTH + MLA_WIDTH], w_out[S5_WIDTH + MLA_WIDTH:]]
    glu = (za, p["s5_glu_w"][i].astype(BF16), p["s5_glu_b"][i].reshape(1, S5_WIDTH))
    return _out_proj(h, [y_s5, y_b, y_m], ws, tm, glu)


def _odd_layer(h, b, s, i, layer, p, mem_k, mem_v, tm):
    w_z, w_xbc, w_dt, w_qm, w_zm = _cols(p["od_w_in"][i], (M2_INNER, M2_CONV_DIM, M2_HEADS, MEM_WIDTH, MEM_WIDTH))
    w_dt = jnp.pad(w_dt, ((0, 0), (0, LANES - M2_HEADS)))
    weights = [w.astype(BF16) for w in (w_z, w_xbc, w_qm, w_zm, w_dt)]
    z, xbc, qm, zm, dt_raw = _rms_proj(h, p["norm_g"][layer], weights, [BF16] * 4 + [F32], tm)
    y_c = _ssd(z.reshape(b, s, M2_INNER), xbc.reshape(b, s, M2_CONV_DIM), dt_raw.reshape(b, s, LANES),
               p["m2_conv_w"][i], p["m2_conv_b"][i], p["m2_dt_bias"][i], p["m2_a_log"][i], p["m2_d"][i],
               p["m2_norm_g"][i]).reshape(b * s, M2_INNER)
    y_m = _mem_attn(qm.reshape(b, s, MEM_WIDTH), zm.reshape(b, s, MEM_WIDTH), mem_k, mem_v,
                    p["mem_q_norm_g"][layer], layer, tm).reshape(b * s, MEM_WIDTH)
    w_out = p["od_w_out"][i].astype(BF16)
    return _out_proj(h, [y_c, y_m], [w_out[:M2_INNER], w_out[M2_INNER:]], tm)


def _token_tile(s):
    return 512 if s % 512 == 0 else s


def kernel(x, mem, positions, norm_g, mem_norm_g, mem_w_kv, mem_q_norm_g, mem_k_norm_g, ev_w_in, ev_w_out, s5_a_re, s5_a_im, s5_log_dt, s5_b_re, s5_b_im, s5_c_re, s5_c_im, s5_d, s5_glu_w, s5_glu_b, mla_q_a_norm_g, mla_w_uq, mla_kv_a_norm_g, mla_w_ukv, mla_q_norm_g, mla_k_norm_g, od_w_in, od_w_out, m2_conv_w, m2_conv_b, m2_dt_bias, m2_a_log, m2_d, m2_norm_g):
    p = dict(norm_g=norm_g, mem_q_norm_g=mem_q_norm_g, ev_w_in=ev_w_in, ev_w_out=ev_w_out,
             s5_a_re=s5_a_re, s5_a_im=s5_a_im, s5_log_dt=s5_log_dt, s5_b_re=s5_b_re, s5_b_im=s5_b_im,
             s5_c_re=s5_c_re, s5_c_im=s5_c_im, s5_d=s5_d, s5_glu_w=s5_glu_w, s5_glu_b=s5_glu_b,
             mla_q_a_norm_g=mla_q_a_norm_g, mla_w_uq=mla_w_uq, mla_kv_a_norm_g=mla_kv_a_norm_g,
             mla_w_ukv=mla_w_ukv, mla_q_norm_g=mla_q_norm_g, mla_k_norm_g=mla_k_norm_g,
             od_w_in=od_w_in, od_w_out=od_w_out, m2_conv_w=m2_conv_w, m2_conv_b=m2_conv_b,
             m2_dt_bias=m2_dt_bias, m2_a_log=m2_a_log, m2_d=m2_d, m2_norm_g=m2_norm_g)
    b, s, d = x.shape
    tm = _token_tile(s)
    tq = 1024 if s % 1024 == 0 else s
    mem_k, mem_v = _mem_kv(mem, mem_norm_g, mem_w_kv, mem_k_norm_g)
    rope_t = _rope_tables(positions)
    h = x.reshape(b * s, d)
    for layer in range(DEPTH):
        if layer % 2 == 0:
            h = _even_layer(h, b, s, layer // 2, layer, p, mem_k, mem_v, rope_t, tm, tq)
        else:
            h = _odd_layer(h, b, s, layer // 2, layer, p, mem_k, mem_v, tm)
    return h.reshape(b, s, d)
```

```python
import functools
import math

import jax
import jax.numpy as jnp
from jax import lax
from jax.experimental import pallas as pl
from jax.experimental.pallas import tpu as pltpu

F32 = jnp.float32
BF16 = jnp.bfloat16

D_MODEL = 1024
DEPTH = 4
CHUNK = 64
N_MEM = 256
RMS_EPS = 1e-6

S5_WIDTH = 512
S5_GROUP = 16
S5_GROUPS = S5_WIDTH // S5_GROUP
S5_STATE = 64
S5_L = 16
S5_FLAT = S5_L * S5_GROUP

MLA_HEADS = 8
MLA_NOPE = 64
MLA_ROPE = 32
MLA_QK = MLA_NOPE + MLA_ROPE
MLA_V = 64
MLA_WIDTH = MLA_HEADS * MLA_V
MLA_VA = MLA_V + 16
MLA_Q_LORA = 256
MLA_KV_LORA = 128
ROPE_BASE = 10000.0
ROPE_HALF = MLA_ROPE // 2
LANES = 128

M2_INNER = 2 * D_MODEL
M2_HEADDIM = 64
M2_HEADS = M2_INNER // M2_HEADDIM
M2_GROUPS = 4
M2_STATE = 128
M2_CONV = 4
M2_CHUNK = 128
M2_CONV_DIM = M2_INNER + 2 * M2_GROUPS * M2_STATE
M2_GW = M2_INNER // M2_GROUPS

MEM_HEADS = 4
MEM_HD = 128
MEM_WIDTH = MEM_HEADS * MEM_HD

VMEM_LIMIT = 56 * 1024 * 1024


def _cparams(sem):
    return pltpu.CompilerParams(dimension_semantics=sem, vmem_limit_bytes=VMEM_LIMIT)


def _silu(z):
    h = 0.5 * z
    return h + h * jnp.tanh(h)


CONV_PAD = 8


def _rms_proj_kernel(h_ref, g_ref, *refs, n_out, col_chunk, conv_idx, tiles_per_seq):
    w_refs = refs[:n_out]
    if conv_idx is None:
        o_refs = refs[n_out:]
    else:
        cw_ref, cb_ref = refs[n_out:n_out + 2]
        o_refs = refs[n_out + 2:2 * n_out + 2]
        pad, tail = refs[2 * n_out + 2:]

        @pl.when(pl.program_id(0) % tiles_per_seq == 0)
        def _():
            tail[...] = jnp.zeros_like(tail)

    x = h_ref[...]
    tm = x.shape[0]
    xn = (x * lax.rsqrt(jnp.mean(x * x, axis=-1, keepdims=True) + RMS_EPS) * g_ref[...]).astype(BF16)
    for i, (w_ref, o_ref) in enumerate(zip(w_refs, o_refs)):
        n = w_ref.shape[1]
        for c0 in range(0, n, col_chunk):
            c1 = min(n, c0 + col_chunk)
            r = jnp.dot(xn, w_ref[:, c0:c1], preferred_element_type=F32)
            if i == conv_idx:
                pad[0:CONV_PAD, :] = tail[:, c0:c1]
                pad[CONV_PAD:CONV_PAD + tm, :] = r
                first = CONV_PAD - (M2_CONV - 1)
                acc = cb_ref[:, c0:c1] + cw_ref[0:1, c0:c1] * pad[first:first + tm, :]
                for k in range(1, M2_CONV):
                    acc = acc + cw_ref[k:k + 1, c0:c1] * pad[first + k:first + k + tm, :]
                tail[:, c0:c1] = pad[tm:tm + CONV_PAD, :]
                r = _silu(acc)
            o_ref[:, c0:c1] = r.astype(o_ref.dtype)


def _rms_proj(h, g, weights, out_dtypes, tm, conv=None, seq_len=None):
    t, d = h.shape
    col_chunk = 512
    fixed = lambda i: (0, 0)
    in_specs = [pl.BlockSpec((tm, d), lambda i: (i, 0)), pl.BlockSpec((1, d), fixed)]
    in_specs += [pl.BlockSpec(w.shape, fixed) for w in weights]
    args = [h, g.reshape(1, d), *weights]
    scratch, conv_idx, tiles_per_seq = [], None, None
    if conv is not None:
        conv_idx, cw, cb = conv
        n = weights[conv_idx].shape[1]
        in_specs += [pl.BlockSpec(cw.shape, fixed), pl.BlockSpec((1, n), fixed)]
        args += [cw, cb.reshape(1, n)]
        scratch = [pltpu.VMEM((tm + CONV_PAD, col_chunk), F32), pltpu.VMEM((CONV_PAD, n), F32)]
        tiles_per_seq = seq_len // tm
    out_specs = [pl.BlockSpec((tm, w.shape[1]), lambda i: (i, 0)) for w in weights]
    out_shape = [jax.ShapeDtypeStruct((t, w.shape[1]), dt) for w, dt in zip(weights, out_dtypes)]
    return pl.pallas_call(
        functools.partial(_rms_proj_kernel, n_out=len(weights), col_chunk=col_chunk, conv_idx=conv_idx,
                          tiles_per_seq=tiles_per_seq),
        grid=(t // tm,), in_specs=in_specs, out_specs=out_specs, out_shape=out_shape, scratch_shapes=scratch,
        compiler_params=_cparams(("arbitrary",)), name="rms_proj",
    )(*args)


def _out_proj_kernel(h_ref, *refs, n_in, pre):
    x_refs, w_refs = refs[:n_in], refs[n_in:2 * n_in]
    extra = refs[2 * n_in:-1]
    o_ref = refs[-1]
    acc = h_ref[...]
    for i, (x_ref, w_ref) in enumerate(zip(x_refs, w_refs)):
        if i == 0 and pre == "m2":
            z_ref, ng_ref = extra
            for g in range(M2_GROUPS):
                gs = slice(g * M2_GW, (g + 1) * M2_GW)
                gated = x_ref[:, gs].astype(F32) * _silu(z_ref[:, gs].astype(F32))
                xn = gated * lax.rsqrt(jnp.mean(gated * gated, axis=-1, keepdims=True) + RMS_EPS) * ng_ref[:, gs]
                acc = acc + jnp.dot(xn.astype(BF16), w_ref[gs, :], preferred_element_type=F32)
            continue
        x = x_ref[...]
        if i == 0 and pre == "glu":
            za_ref, gw_ref, gb_ref = extra
            gate = jnp.dot(x, gw_ref[...], preferred_element_type=F32) + gb_ref[...]
            x = (x.astype(F32) * jax.nn.sigmoid(gate) * _silu(za_ref[...].astype(F32))).astype(BF16)
        acc = acc + jnp.dot(x, w_ref[...], preferred_element_type=F32)
    o_ref[...] = acc


def _out_proj(h, xs, ws, tm, glu_args=None, m2_args=None):
    t, d = h.shape
    row = lambda i: (i, 0)
    fixed = lambda i: (0, 0)
    in_specs = [pl.BlockSpec((tm, d), row)]
    in_specs += [pl.BlockSpec((tm, x.shape[1]), row) for x in xs]
    in_specs += [pl.BlockSpec(w.shape, fixed) for w in ws]
    args = [h, *xs, *ws]
    if glu_args is not None:
        za, gw, gb = glu_args
        in_specs += [pl.BlockSpec((tm, za.shape[1]), row), pl.BlockSpec(gw.shape, fixed),
                     pl.BlockSpec(gb.shape, fixed)]
        args += [za, gw, gb]
    if m2_args is not None:
        z, ng = m2_args
        in_specs += [pl.BlockSpec((tm, z.shape[1]), row), pl.BlockSpec(ng.shape, fixed)]
        args += [z, ng]
    pre = "glu" if glu_args is not None else "m2" if m2_args is not None else None
    return pl.pallas_call(
        functools.partial(_out_proj_kernel, n_in=len(xs), pre=pre),
        grid=(t // tm,), in_specs=in_specs, out_specs=pl.BlockSpec((tm, d), row),
        out_shape=jax.ShapeDtypeStruct((t, d), F32),
        compiler_params=_cparams(("parallel",)), name="out_proj",
    )(*args)


def _mem_kv_kernel(mem_ref, g_ref, w_ref, kg_ref, k_ref, v_ref):
    x = mem_ref[0]
    xn = (x * lax.rsqrt(jnp.mean(x * x, axis=-1, keepdims=True) + RMS_EPS) * g_ref[0]).astype(BF16)
    kv = jnp.dot(xn, w_ref[0], preferred_element_type=F32)
    for h in range(MEM_HEADS):
        kh = kv[:, h * MEM_HD:(h + 1) * MEM_HD]
        kn = kh * lax.rsqrt(jnp.mean(kh * kh, axis=-1, keepdims=True) + RMS_EPS) * kg_ref[0]
        k_ref[0, 0, :, h * MEM_HD:(h + 1) * MEM_HD] = kn.astype(BF16)
    v_ref[0, 0] = kv[:, MEM_WIDTH:].astype(BF16)


def _mem_kv(mem, mem_norm_g, w_kv, k_norm_g):
    b = mem.shape[0]
    out = jax.ShapeDtypeStruct((DEPTH, b, N_MEM, MEM_WIDTH), BF16)
    return pl.pallas_call(
        _mem_kv_kernel, grid=(DEPTH, b),
        in_specs=[pl.BlockSpec((1, N_MEM, D_MODEL), lambda l, i: (i, 0, 0)),
                  pl.BlockSpec((1, 1, D_MODEL), lambda l, i: (l, 0, 0)),
                  pl.BlockSpec((1, D_MODEL, 2 * MEM_WIDTH), lambda l, i: (l, 0, 0)),
                  pl.BlockSpec((1, 1, MEM_HD), lambda l, i: (l, 0, 0))],
        out_specs=[pl.BlockSpec((1, 1, N_MEM, MEM_WIDTH), lambda l, i: (l, i, 0, 0))] * 2,
        out_shape=[out, out], compiler_params=_cparams(("arbitrary", "arbitrary")), name="mem_kv",
    )(mem, mem_norm_g.reshape(DEPTH, 1, D_MODEL), w_kv.astype(BF16), k_norm_g.reshape(DEPTH, 1, MEM_HD))


def _mem_attn_kernel(q_ref, z_ref, k_ref, v_ref, qg_ref, o_ref):
    scale = 1.0 / math.sqrt(MEM_HD)
    for h in range(MEM_HEADS):
        sl = slice(h * MEM_HD, (h + 1) * MEM_HD)
        q = q_ref[0, :, sl].astype(F32)
        qn = (q * lax.rsqrt(jnp.mean(q * q, axis=-1, keepdims=True) + RMS_EPS) * (qg_ref[...] * scale)).astype(BF16)
        s = lax.dot_general(qn, k_ref[0, 0, :, sl], (((1,), (1,)), ((), ())), preferred_element_type=F32)
        p = jnp.exp(s - jnp.max(s, axis=-1, keepdims=True))
        l = jnp.sum(p, axis=-1, keepdims=True)
        o = jnp.dot(p.astype(BF16), v_ref[0, 0, :, sl], preferred_element_type=F32) / l
        o_ref[0, :, sl] = (o * _silu(z_ref[0, :, sl].astype(F32))).astype(BF16)


def _mem_attn(q, z, k_all, v_all, qg, layer, tq):
    b, s, _ = q.shape
    tok = pl.BlockSpec((1, tq, MEM_WIDTH), lambda i, j: (i, j, 0))
    bank = pl.BlockSpec((1, 1, N_MEM, MEM_WIDTH), lambda i, j: (layer, i, 0, 0))
    return pl.pallas_call(
        _mem_attn_kernel, grid=(b, s // tq),
        in_specs=[tok, tok, bank, bank, pl.BlockSpec((1, MEM_HD), lambda i, j: (0, 0))],
        out_specs=tok, out_shape=jax.ShapeDtypeStruct((b, s, MEM_WIDTH), BF16),
        compiler_params=_cparams(("parallel", "parallel")), name="mem_attn",
    )(q, z, k_all, v_all, qg.reshape(1, MEM_HD))


def _rope_table_kernel(pos_ref, inv_ref, cos_ref, sin_ref):
    ang = pos_ref[0].astype(F32) * inv_ref[...]
    cos_ref[0] = jnp.cos(ang)
    sin_ref[0] = jnp.sin(ang)


def _rope_tables(positions):
    b, s = positions.shape
    inv = ROPE_BASE ** (-jnp.arange(ROPE_HALF, dtype=F32) / ROPE_HALF)
    out = jax.ShapeDtypeStruct((b, ROPE_HALF, s), F32)
    cos, sin = pl.pallas_call(
        _rope_table_kernel, grid=(b,),
        in_specs=[pl.BlockSpec((1, 1, s), lambda i: (i, 0, 0)), pl.BlockSpec((ROPE_HALF, 1), lambda i: (0, 0))],
        out_specs=[pl.BlockSpec((1, ROPE_HALF, s), lambda i: (i, 0, 0))] * 2,
        out_shape=[out, out], compiler_params=_cparams(("parallel",)), name="rope_tables",
    )(positions.reshape(b, 1, s), inv.reshape(ROPE_HALF, 1))
    tail = LANES - MLA_QK
    cos_q = jnp.concatenate([jnp.ones((b, MLA_NOPE, s), F32), cos, cos, jnp.ones((b, tail, s), F32)], 1)
    sin_q = jnp.concatenate([jnp.zeros((b, MLA_NOPE, s), F32), -sin, sin, jnp.zeros((b, tail, s), F32)], 1)
    cos_k = jnp.swapaxes(cos_q, 1, 2).reshape(b * s, LANES)
    sin_k = jnp.swapaxes(sin_q, 1, 2).reshape(b * s, LANES)
    return cos_q, sin_q, cos_k, sin_k


def _head_norm_rope(x, gain, cos, sin, lane):
    xn = x * lax.rsqrt(jnp.sum(x * x, axis=-1, keepdims=True) * (1.0 / MLA_QK) + RMS_EPS) * gain
    partner = jnp.where(lane < MLA_NOPE + ROPE_HALF,
                        pltpu.roll(xn, LANES - ROPE_HALF, 1), pltpu.roll(xn, ROPE_HALF, 1))
    return xn * cos + partner * sin


def _mla_prep_kernel(cq_ref, ckv_ref, kr_ref, cosq_ref, sinq_ref, cosk_ref, sink_ref, gqa_ref, wqt_ref, gkva_ref,
                     wk_ref, wvt_ref, gq_ref, gk_ref, qt_ref, k_ref, vt_ref):
    cq = cq_ref[...]
    cqn = cq * lax.rsqrt(jnp.mean(cq * cq, axis=-1, keepdims=True) + RMS_EPS) * gqa_ref[...]
    ckv = ckv_ref[...]
    ckvn = ckv * lax.rsqrt(jnp.mean(ckv * ckv, axis=-1, keepdims=True) + RMS_EPS) * gkva_ref[...]
    cqn_t = cqn.T.astype(BF16)
    ckvn_t = ckvn.T.astype(BF16)
    ckvn = ckvn.astype(BF16)
    kr = kr_ref[...]
    cosq, sinq, cosk, sink = cosq_ref[0], sinq_ref[0], cosk_ref[...], sink_ref[...]
    lane = lax.broadcasted_iota(jnp.int32, cosk.shape, 1)
    row = lax.broadcasted_iota(jnp.int32, cosq.shape, 0)
    qscale = math.log2(math.e) / math.sqrt(MLA_QK)
    tm = cq.shape[0]
    ones_tile = jnp.where(lax.broadcasted_iota(jnp.int32, (MLA_VA - MLA_V, tm), 0) == 0, 1.0, 0.0).astype(BF16)
    for h in range(MLA_HEADS):
        sl = slice(h * LANES, (h + 1) * LANES)
        qt = jnp.dot(wqt_ref[sl, :], cqn_t, preferred_element_type=F32)
        qn = qt * lax.rsqrt(jnp.sum(qt * qt, axis=0, keepdims=True) * (1.0 / MLA_QK) + RMS_EPS) * gq_ref[...]
        partner = jnp.where(row < MLA_NOPE + ROPE_HALF,
                            pltpu.roll(qn, LANES - ROPE_HALF, 0), pltpu.roll(qn, ROPE_HALF, 0))
        qt_ref[0, 0, sl, :] = ((qn * cosq + partner * sinq) * qscale).astype(BF16)
        kh = jnp.dot(ckvn, wk_ref[:, sl], preferred_element_type=F32) + kr
        k_ref[:, sl] = _head_norm_rope(kh, gk_ref[...], cosk, sink, lane).astype(BF16)
        vs = slice(h * MLA_V, (h + 1) * MLA_V)
        vt_ref[0, 0, h * MLA_VA:h * MLA_VA + MLA_V, :] = jnp.dot(wvt_ref[vs, :], ckvn_t,
                                                                 preferred_element_type=F32).astype(BF16)
        vt_ref[0, 0, h * MLA_VA + MLA_V:(h + 1) * MLA_VA, :] = ones_tile


def _pad_heads(w, n_heads, width, offset=0):
    k = w.shape[0]
    w = w.reshape(k, n_heads, width)
    w = jnp.pad(w, ((0, 0), (0, 0), (offset, LANES - width - offset)))
    return w.reshape(k, n_heads * LANES)


def _mla_prep(cq, ckv, kr, rope_t, gqa, w_uq, gkva, w_ukv, gq, gk, b, s, tm):
    cos_q, sin_q, cos_k, sin_k = rope_t
    nt = s // tm
    wqt = _pad_heads(w_uq, MLA_HEADS, MLA_QK).T.astype(BF16)
    w_ukv = w_ukv.reshape(MLA_KV_LORA, MLA_HEADS, MLA_NOPE + MLA_V)
    wk = _pad_heads(w_ukv[:, :, :MLA_NOPE].reshape(MLA_KV_LORA, -1), MLA_HEADS, MLA_NOPE).astype(BF16)
    wvt = w_ukv[:, :, MLA_NOPE:].reshape(MLA_KV_LORA, MLA_WIDTH).T.astype(BF16)
    pad_gain = lambda g: jnp.pad(g, (0, LANES - MLA_QK))
    row = lambda i, j: (i * nt + j, 0)
    fixed = lambda i, j: (0, 0)
    hw = MLA_HEADS * LANES
    return pl.pallas_call(
        _mla_prep_kernel, grid=(b, nt),
        in_specs=[pl.BlockSpec((tm, MLA_Q_LORA), row), pl.BlockSpec((tm, MLA_KV_LORA), row),
                  pl.BlockSpec((tm, LANES), row),
                  pl.BlockSpec((1, LANES, tm), lambda i, j: (i, 0, j)), pl.BlockSpec((1, LANES, tm), lambda i, j: (i, 0, j)),
                  pl.BlockSpec((tm, LANES), row), pl.BlockSpec((tm, LANES), row),
                  pl.BlockSpec((1, MLA_Q_LORA), fixed), pl.BlockSpec((hw, MLA_Q_LORA), fixed),
                  pl.BlockSpec((1, MLA_KV_LORA), fixed), pl.BlockSpec((MLA_KV_LORA, hw), fixed),
                  pl.BlockSpec((MLA_WIDTH, MLA_KV_LORA), fixed),
                  pl.BlockSpec((LANES, 1), fixed), pl.BlockSpec((1, LANES), fixed)],
        out_specs=[pl.BlockSpec((1, 1, hw, tm), lambda i, j: (i, j, 0, 0)), pl.BlockSpec((tm, hw), row),
                   pl.BlockSpec((1, 1, MLA_HEADS * MLA_VA, tm), lambda i, j: (i, j, 0, 0))],
        out_shape=[jax.ShapeDtypeStruct((b, nt, hw, tm), BF16), jax.ShapeDtypeStruct((b * s, hw), BF16),
                   jax.ShapeDtypeStruct((b, nt, MLA_HEADS * MLA_VA, tm), BF16)],
        compiler_params=_cparams(("parallel", "parallel")), name="mla_prep",
    )(cq, ckv, kr, cos_q, sin_q, cos_k, sin_k, gqa.reshape(1, -1), wqt, gkva.reshape(1, -1), wk, wvt,
      pad_gain(gq).reshape(LANES, 1), pad_gain(gk).reshape(1, LANES))


def _mla_attn_kernel(q_ref, k_ref, v_ref, z_ref, o_ref, m_sc, acc_sc, *, tq):
    qi = pl.program_id(2)
    m_sc[...] = jnp.full(m_sc.shape, -jnp.inf, F32)
    acc_sc[...] = jnp.zeros(acc_sc.shape, F32)

    def block(ki, masked):
        start = pl.multiple_of(ki * tq, tq)
        for h in range(2):
            s = jnp.dot(k_ref[0, pl.ds(start, tq), h * LANES:(h + 1) * LANES], q_ref[0, 0, h * LANES:(h + 1) * LANES, :],
                        preferred_element_type=F32)
            if masked:
                kc = lax.broadcasted_iota(jnp.int32, s.shape, 0) // CHUNK
                qc = lax.broadcasted_iota(jnp.int32, s.shape, 1) // CHUNK
                s = jnp.where(kc <= qc, s, -jnp.inf)
            m_prev = m_sc[h]
            m_new = jnp.maximum(m_prev, jnp.max(s, axis=0, keepdims=True))
            alpha = jnp.exp2(m_prev - m_new)
            p = jnp.exp2(s - m_new)
            acc_sc[h] = alpha * acc_sc[h] + jnp.dot(v_ref[0, ki, h * MLA_VA:(h + 1) * MLA_VA, :], p.astype(BF16),
                                                    preferred_element_type=F32)
            m_sc[h] = m_new

    def body(ki, carry):
        block(ki, masked=False)
        return carry

    lax.fori_loop(0, qi, body, 0)
    block(qi, masked=True)
    o_t = jnp.concatenate([acc_sc[h, :MLA_V, :] / acc_sc[h, MLA_V:MLA_V + 1, :] for h in range(2)], axis=0)
    o_ref[0] = (o_t.T * _silu(z_ref[0].astype(F32))).astype(BF16)


def _mla_attn(qt, k, vt, z, tq):
    b, s, _ = k.shape
    nt = s // tq
    pairs = MLA_HEADS // 2
    return pl.pallas_call(
        functools.partial(_mla_attn_kernel, tq=tq), grid=(b, pairs, nt),
        in_specs=[pl.BlockSpec((1, 1, 2 * LANES, tq), lambda i, p, j: (i, j, p, 0)),
                  pl.BlockSpec((1, s, 2 * LANES), lambda i, p, j: (i, 0, p)),
                  pl.BlockSpec((1, nt, 2 * MLA_VA, tq), lambda i, p, j: (i, 0, p, 0)),
                  pl.BlockSpec((1, tq, LANES), lambda i, p, j: (i, j, p))],
        out_specs=pl.BlockSpec((1, tq, LANES), lambda i, p, j: (i, j, p)),
        out_shape=jax.ShapeDtypeStruct((b, s, MLA_WIDTH), BF16),
        scratch_shapes=[pltpu.VMEM((2, 1, tq), F32), pltpu.VMEM((2, MLA_VA, tq), F32)],
        compiler_params=_cparams(("parallel", "parallel", "arbitrary")), name="mla_attn",
    )(qt, k, vt, z)


def _s5_tables(a_re, a_im, log_dt, b_re, b_im, c_re, c_im, n_steps):
    hp = lax.Precision.HIGHEST
    dt = jnp.exp(log_dt)[:, None]
    mag = jnp.exp(a_re * dt)
    ab_re, ab_im = mag * jnp.cos(a_im * dt), mag * jnp.sin(a_im * dt)
    den = a_re * a_re + a_im * a_im
    n_re, n_im = ab_re - 1.0, ab_im
    f_re = (n_re * a_re + n_im * a_im) / den
    f_im = (n_im * a_re - n_re * a_im) / den
    bb_re = f_re[..., None] * b_re - f_im[..., None] * b_im
    bb_im = f_re[..., None] * b_im + f_im[..., None] * b_re

    def lam_pow(k):
        k = k.astype(F32)[:, None, None]
        m = jnp.exp(k * a_re * dt)
        return m * jnp.cos(k * a_im * dt), m * jnp.sin(k * a_im * dt)

    L, C, P, G = S5_L, S5_GROUP, S5_STATE, S5_GROUPS
    lr, li = lam_pow(jnp.arange(L + 1))
    w_re = lr[..., None] * bb_re[None] - li[..., None] * bb_im[None]
    w_im = lr[..., None] * bb_im[None] + li[..., None] * bb_re[None]
    kk = (jnp.einsum("gdp,kgpc->kgdc", c_re, w_re[:L], precision=hp)
          - jnp.einsum("gdp,kgpc->kgdc", c_im, w_im[:L], precision=hp))
    lag = jnp.arange(L)[None, :] - jnp.arange(L)[:, None]
    kfull = jnp.where((lag >= 0)[:, :, None, None, None], kk[jnp.clip(lag, 0, L - 1)], 0.0)
    mt = jnp.transpose(kfull, (2, 0, 4, 1, 3)).reshape(G, L * C, L * C)
    e_re = jnp.transpose(w_re[:L][::-1], (1, 0, 3, 2)).reshape(G, L * C, P)
    e_im = jnp.transpose(w_im[:L][::-1], (1, 0, 3, 2)).reshape(G, L * C, P)
    w1 = jnp.concatenate([mt, e_re, e_im, -e_im, e_re], axis=-1)
    g_re = c_re[None] * lr[1:, :, None, :] - c_im[None] * li[1:, :, None, :]
    g_im = c_re[None] * li[1:, :, None, :] + c_im[None] * lr[1:, :, None, :]
    f_mat = jnp.concatenate([jnp.transpose(g_re, (1, 3, 0, 2)).reshape(G, P, L * C),
                             -jnp.transpose(g_im, (1, 3, 0, 2)).reshape(G, P, L * C)], axis=1)
    sr, si = lam_pow(L * (2 ** jnp.arange(n_steps)))
    la = jnp.transpose(jnp.concatenate([sr, sr], -1), (1, 0, 2))[:, :, None, :]
    lb = jnp.transpose(jnp.concatenate([si, si], -1), (1, 0, 2))[:, :, None, :]
    return w1.astype(BF16), f_mat.astype(BF16), la, lb


def _s5_kernel(u_ref, w1_ref, f_ref, la_ref, lb_ref, d_ref, y_ref, *, gb, n_steps):
    nc = u_ref.shape[2]
    row = lax.broadcasted_iota(jnp.int32, (nc, 2 * S5_STATE), 0)

    def shift(t, d):
        return jnp.where(row >= d, pltpu.roll(t, d, 0), 0.0)

    for g in range(gb):
        u = u_ref[0, g]
        r = jnp.dot(u, w1_ref[g], preferred_element_type=F32)
        y = r[:, :S5_FLAT]
        x = r[:, S5_FLAT:S5_FLAT + 2 * S5_STATE]
        xs = r[:, S5_FLAT + 2 * S5_STATE:]
        for s in range(n_steps):
            a, b = la_ref[g, s], lb_ref[g, s]
            tx, txs = a * x + b * xs, a * xs - b * x
            x, xs = x + shift(tx, 1 << s), xs + shift(txs, 1 << s)
        h_in = shift(x, 1).astype(BF16)
        y = y + jnp.dot(h_in, f_ref[g], preferred_element_type=F32) + d_ref[g] * u.astype(F32)
        y_ref[0, g] = jax.nn.gelu(y, approximate=True).astype(BF16)


def _s5(u, tables, d, gb):
    w1, f_mat, la, lb = tables
    b, s, _ = u.shape
    nc = s // S5_L
    n_steps = la.shape[1]
    G = S5_GROUPS
    uf = u.reshape(b, nc, S5_L, G, S5_GROUP).transpose(0, 3, 1, 2, 4).reshape(b, G, nc, S5_FLAT)
    d_flat = jnp.tile(d.reshape(G, 1, S5_GROUP), (1, 1, S5_L))
    grp = lambda i, j: (j, 0, 0)
    y = pl.pallas_call(
        functools.partial(_s5_kernel, gb=gb, n_steps=n_steps), grid=(b, G // gb),
        in_specs=[pl.BlockSpec((1, gb, nc, S5_FLAT), lambda i, j: (i, j, 0, 0)),
                  pl.BlockSpec((gb, S5_FLAT, 2 * S5_FLAT), grp),
                  pl.BlockSpec((gb, 2 * S5_STATE, S5_FLAT), grp),
                  pl.BlockSpec((gb, n_steps, 1, 2 * S5_STATE), lambda i, j: (j, 0, 0, 0)),
                  pl.BlockSpec((gb, n_steps, 1, 2 * S5_STATE), lambda i, j: (j, 0, 0, 0)),
                  pl.BlockSpec((gb, 1, S5_FLAT), grp)],
        out_specs=pl.BlockSpec((1, gb, nc, S5_FLAT), lambda i, j: (i, j, 0, 0)),
        out_shape=jax.ShapeDtypeStruct((b, G, nc, S5_FLAT), BF16),
        compiler_params=_cparams(("parallel", "parallel")), name="s5",
    )(uf, w1, f_mat, la, lb, d_flat)
    return y.reshape(b, G, nc, S5_L, S5_GROUP).transpose(0, 2, 3, 1, 4).reshape(b, s, S5_WIDTH)


def _split_dot(x, e):
    hi = x.astype(BF16)
    lo = (x - hi.astype(F32)).astype(BF16)
    return jnp.dot(hi, e, preferred_element_type=F32) + jnp.dot(lo, e, preferred_element_type=F32)


def _ssd_kernel(xc_ref, dt_ref, dtb_ref, alog_ref, dexp_ref, e_ref, y_ref, state):
    L, PAD = M2_CHUNK, 8

    @pl.when(pl.program_id(1) == 0)
    def _():
        state[...] = jnp.zeros_like(state)

    xc = xc_ref[0].astype(F32)
    xs = xc[:, :M2_INNER]
    nbc = M2_GROUPS * M2_STATE
    bm = xc[:, M2_INNER:M2_INNER + nbc]
    cm = xc[:, M2_INNER + nbc:]

    dt = jax.nn.softplus(dt_ref[0] + dtb_ref[...])
    da = dt * (-jnp.exp(alog_ref[...]))
    ti = lax.broadcasted_iota(jnp.int32, (L, L), 0)
    si = lax.broadcasted_iota(jnp.int32, (L, L), 1)
    causal = si <= ti
    tril =jnp.where(causal, 1.0, 0.0).astype(BF16)
    hi = da.astype(BF16)
    r1 = da - hi.astype(F32)
    mid = r1.astype(BF16)
    lo = (r1 - mid.astype(F32)).astype(BF16)
    cs = (jnp.dot(tril, hi, preferred_element_type=F32) + jnp.dot(tril, mid, preferred_element_type=F32)
          + jnp.dot(tril, lo, preferred_element_type=F32))
    cs2 = cs * math.log2(math.e)
    cs2_t = cs2.T
    cs_end = cs[L - 1:L, :]
    stack = jnp.concatenate([dt, dt * jnp.exp(cs_end - cs), jnp.exp(cs)], axis=0).astype(BF16)
    fac = jnp.dot(stack, e_ref[...], preferred_element_type=F32)
    dt_e, dw_e, ecs_e = fac[:L], fac[L:2 * L], fac[2 * L:]
    dend_e = _split_dot(jnp.broadcast_to(jnp.exp(cs_end), (PAD, LANES)), e_ref[...])[0:1]
    x_dt = (xs * dt_e).astype(BF16)
    x_w = (xs * dw_e).astype(BF16)

    lane = lax.broadcasted_iota(jnp.int32, (L, LANES), 1)
    zero = jnp.zeros((L, LANES), BF16)
    for g in range(M2_GROUPS):
        gs = slice(g * M2_GW, (g + 1) * M2_GW)
        b_g = bm[:, g * M2_STATE:(g + 1) * M2_STATE]
        c_g = cm[:, g * M2_STATE:(g + 1) * M2_STATE].astype(BF16)
        cb = lax.dot_general(c_g, b_g.astype(BF16), (((1,), (1,)), ((), ())), preferred_element_type=F32)
        s_old = state[g]
        y_off = jnp.dot(c_g, s_old.astype(BF16), preferred_element_type=F32) * ecs_e[:, gs]
        state[g] = s_old * dend_e[:, gs] + jnp.dot(b_g.T.astype(BF16), x_w[:, gs], preferred_element_type=F32)
        for pr in range(M2_GW // LANES):
            ps = slice(g * M2_GW + pr * LANES, g * M2_GW + (pr + 1) * LANES)
            xp = x_dt[:, ps]
            y_pair = None
            for hh in range(2):
                h = (g * M2_GW + pr * LANES) // M2_HEADDIM + hh
                seg = jnp.exp2(jnp.where(causal, cs2[:, h:h + 1] - cs2_t[h:h + 1, :], -jnp.inf))
                att = (cb * seg).astype(BF16)
                x_h = jnp.where((lane < M2_HEADDIM) == (hh == 0), xp, zero)
                part = jnp.dot(att, x_h, preferred_element_type=F32)
                y_pair = part if y_pair is None else y_pair + part
            y = y_pair + y_off[:, pr * LANES:(pr + 1) * LANES] + dexp_ref[:, ps] * xs[:, ps]
            y_ref[0, :, ps] = y.astype(y_ref.dtype)


def _ssd(xc, dt_raw, dt_bias, a_log, d):
    b, s, _ = xc.shape
    pad_row = lambda v: jnp.pad(v, (0, LANES - M2_HEADS)).reshape(1, LANES)
    expand = (jnp.arange(LANES)[:, None] == (jnp.arange(M2_INNER) // M2_HEADDIM)[None, :]).astype(BF16)
    tok = lambda w: pl.BlockSpec((1, M2_CHUNK, w), lambda i, j: (i, j, 0))
    fixed = lambda shape: pl.BlockSpec(shape, lambda i, j: (0, 0))
    return pl.pallas_call(
        _ssd_kernel, grid=(b, s // M2_CHUNK),
        in_specs=[tok(M2_CONV_DIM), tok(LANES), fixed((1, LANES)), fixed((1, LANES)), fixed((1, M2_INNER)),
                  fixed((LANES, M2_INNER))],
        out_specs=tok(M2_INNER), out_shape=jax.ShapeDtypeStruct((b, s, M2_INNER), BF16),
        scratch_shapes=[pltpu.VMEM((M2_GROUPS, M2_STATE, M2_GW), F32)],
        compiler_params=_cparams(("parallel", "arbitrary")), name="ssd",
    )(xc, dt_raw, pad_row(dt_bias), pad_row(a_log), jnp.repeat(d, M2_HEADDIM).reshape(1, M2_INNER), expand)


def _cols(w, sizes):
    idx, out = 0, []
    for n in sizes:
        out.append(w[:, idx:idx + n])
        idx += n
    return out


def _even_layer(h, b, s, i, layer, p, mem_k, mem_v, rope_t, tm, tq):
    w_u, w_za, w_cq, w_ckv, w_kr, w_zb, w_qm, w_zm = _cols(
        p["ev_w_in"][i], (S5_WIDTH, S5_WIDTH, MLA_Q_LORA, MLA_KV_LORA, MLA_ROPE, MLA_WIDTH, MEM_WIDTH, MEM_WIDTH))
    w_kr = jnp.pad(w_kr, ((0, 0), (MLA_NOPE, LANES - MLA_QK)))
    weights = [w.astype(BF16) for w in (w_u, w_za, w_zb, w_qm, w_zm, w_cq, w_ckv, w_kr)]
    u, za, zb, qm, zm, cq, ckv, kr = _rms_proj(h, p["norm_g"][layer], weights, [BF16] * 5 + [F32] * 3, tm)

    n_steps = max(1, (s // S5_L - 1).bit_length())
    tables = _s5_tables(p["s5_a_re"][i], p["s5_a_im"][i], p["s5_log_dt"][i], p["s5_b_re"][i], p["s5_b_im"][i],
                        p["s5_c_re"][i], p["s5_c_im"][i], n_steps)
    y_s5 = _s5(u.reshape(b, s, S5_WIDTH), tables, p["s5_d"][i], gb=8).reshape(b * s, S5_WIDTH)

    qt, k, vt = _mla_prep(cq, ckv, kr, rope_t, p["mla_q_a_norm_g"][i], p["mla_w_uq"][i], p["mla_kv_a_norm_g"][i],
                          p["mla_w_ukv"][i], p["mla_q_norm_g"][i], p["mla_k_norm_g"][i], b, s, tq)
    y_b = _mla_attn(qt, k.reshape(b, s, MLA_HEADS * LANES), vt, zb.reshape(b, s, MLA_WIDTH),
                    tq).reshape(b * s, MLA_WIDTH)

    y_m = _mem_attn(qm.reshape(b, s, MEM_WIDTH), zm.reshape(b, s, MEM_WIDTH), mem_k, mem_v,
                    p["mem_q_norm_g"][layer], layer, tm).reshape(b * s, MEM_WIDTH)

    w_out = p["ev_w_out"][i].astype(BF16)
    ws = [w_out[:S5_WIDTH], w_out[S5_WIDTH:S5_WIDTH + MLA_WIDTH], w_out[S5_WIDTH + MLA_WIDTH:]]
    glu = (za, p["s5_glu_w"][i].astype(BF16), p["s5_glu_b"][i].reshape(1, S5_WIDTH))
    return _out_proj(h, [y_s5, y_b, y_m], ws, tm, glu)


def _odd_layer(h, b, s, i, layer, p, mem_k, mem_v, tm):
    w_z, w_xbc, w_dt, w_qm, w_zm = _cols(p["od_w_in"][i], (M2_INNER, M2_CONV_DIM, M2_HEADS, MEM_WIDTH, MEM_WIDTH))
    w_dt = jnp.pad(w_dt, ((0, 0), (0, LANES - M2_HEADS)))
    weights = [w.astype(BF16) for w in (w_z, w_xbc, w_qm, w_zm, w_dt)]
    z, xc, qm, zm, dt_raw = _rms_proj(h, p["norm_g"][layer], weights, [BF16] * 4 + [F32], tm,
                                      conv=(1, p["m2_conv_w"][i], p["m2_conv_b"][i]), seq_len=s)
    y_c = _ssd(xc.reshape(b, s, M2_CONV_DIM), dt_raw.reshape(b, s, LANES), p["m2_dt_bias"][i], p["m2_a_log"][i],
               p["m2_d"][i]).reshape(b * s, M2_INNER)
    y_m = _mem_attn(qm.reshape(b, s, MEM_WIDTH), zm.reshape(b, s, MEM_WIDTH), mem_k, mem_v,
                    p["mem_q_norm_g"][layer], layer, tm).reshape(b * s, MEM_WIDTH)
    w_out = p["od_w_out"][i].astype(BF16)
    return _out_proj(h, [y_c, y_m], [w_out[:M2_INNER], w_out[M2_INNER:]], tm,
                     m2_args=(z, p["m2_norm_g"][i].reshape(1, M2_INNER)))


def _token_tile(s):
    return 512 if s % 512 == 0 else s


def kernel(x, mem, positions, norm_g, mem_norm_g, mem_w_kv, mem_q_norm_g, mem_k_norm_g, ev_w_in, ev_w_out, s5_a_re, s5_a_im, s5_log_dt, s5_b_re, s5_b_im, s5_c_re, s5_c_im, s5_d, s5_glu_w, s5_glu_b, mla_q_a_norm_g, mla_w_uq, mla_kv_a_norm_g, mla_w_ukv, mla_q_norm_g, mla_k_norm_g, od_w_in, od_w_out, m2_conv_w, m2_conv_b, m2_dt_bias, m2_a_log, m2_d, m2_norm_g):
    p = dict(norm_g=norm_g, mem_q_norm_g=mem_q_norm_g, ev_w_in=ev_w_in, ev_w_out=ev_w_out,
             s5_a_re=s5_a_re, s5_a_im=s5_a_im, s5_log_dt=s5_log_dt, s5_b_re=s5_b_re, s5_b_im=s5_b_im,
             s5_c_re=s5_c_re, s5_c_im=s5_c_im, s5_d=s5_d, s5_glu_w=s5_glu_w, s5_glu_b=s5_glu_b,
             mla_q_a_norm_g=mla_q_a_norm_g, mla_w_uq=mla_w_uq, mla_kv_a_norm_g=mla_kv_a_norm_g,
             mla_w_ukv=mla_w_ukv, mla_q_norm_g=mla_q_norm_g, mla_k_norm_g=mla_k_norm_g,
             od_w_in=od_w_in, od_w_out=od_w_out, m2_conv_w=m2_conv_w, m2_conv_b=m2_conv_b,
             m2_dt_bias=m2_dt_bias, m2_a_log=m2_a_log, m2_d=m2_d, m2_norm_g=m2_norm_g)
    b, s, d = x.shape
    tm = _token_tile(s)
    tq = 1024 if s % 1024 == 0 else s
    mem_k, mem_v = _mem_kv(mem, mem_norm_g, mem_w_kv, mem_k_norm_g)
    rope_t = _rope_tables(positions)
    h = x.reshape(b * s, d)
    for layer in range(DEPTH):
        if layer % 2 == 0:
            h = _even_layer(h, b, s, layer // 2, layer, p, mem_k, mem_v, rope_t, tm, tq)
        else:
            h = _odd_layer(h, b, s, layer // 2, layer, p, mem_k, mem_v, tm)
    return h.reshape(b, s, d)
```

```python
import functools
import math

import jax
import jax.numpy as jnp
from jax import lax
from jax.experimental import pallas as pl
from jax.experimental.pallas import tpu as pltpu

F32 = jnp.float32
BF16 = jnp.bfloat16

D_MODEL = 1024
DEPTH = 4
CHUNK = 64
N_MEM = 256
RMS_EPS = 1e-6

S5_WIDTH = 512
S5_GROUP = 16
S5_GROUPS = S5_WIDTH // S5_GROUP
S5_STATE = 64
S5_L = 16
S5_FLAT = S5_L * S5_GROUP

MLA_HEADS = 8
MLA_NOPE = 64
MLA_ROPE = 32
MLA_QK = MLA_NOPE + MLA_ROPE
MLA_V = 64
MLA_WIDTH = MLA_HEADS * MLA_V
MLA_VA = MLA_V + 16
MLA_Q_LORA = 256
MLA_KV_LORA = 128
ROPE_BASE = 10000.0
ROPE_HALF = MLA_ROPE // 2
LANES = 128

M2_INNER = 2 * D_MODEL
M2_HEADDIM = 64
M2_HEADS = M2_INNER // M2_HEADDIM
M2_GROUPS = 4
M2_STATE = 128
M2_CONV = 4
M2_CHUNK = 128
M2_CONV_DIM = M2_INNER + 2 * M2_GROUPS * M2_STATE
M2_GW = M2_INNER // M2_GROUPS

MEM_HEADS = 4
MEM_HD = 128
MEM_WIDTH = MEM_HEADS * MEM_HD

VMEM_LIMIT = 56 * 1024 * 1024


def _cparams(sem):
    return pltpu.CompilerParams(dimension_semantics=sem, vmem_limit_bytes=VMEM_LIMIT)


def _silu(z):
    h = 0.5 * z
    return h + h * jnp.tanh(h)


CONV_PAD = 8


def _rms_proj_kernel(h_ref, g_ref, *refs, n_out, col_chunk, conv_idx, tiles_per_seq):
    w_refs = refs[:n_out]
    if conv_idx is None:
        o_refs = refs[n_out:]
    else:
        cw_ref, cb_ref = refs[n_out:n_out + 2]
        o_refs = refs[n_out + 2:2 * n_out + 2]
        pad, tail = refs[2 * n_out + 2:]

        @pl.when(pl.program_id(0) % tiles_per_seq == 0)
        def _():
            tail[...] = jnp.zeros_like(tail)

    x = h_ref[...]
    tm = x.shape[0]
    xn = (x * lax.rsqrt(jnp.mean(x * x, axis=-1, keepdims=True) + RMS_EPS) * g_ref[...]).astype(BF16)
    for i, (w_ref, o_ref) in enumerate(zip(w_refs, o_refs)):
        n = w_ref.shape[1]
        for c0 in range(0, n, col_chunk):
            c1 = min(n, c0 + col_chunk)
            r = jnp.dot(xn, w_ref[:, c0:c1], preferred_element_type=F32)
            if i == conv_idx:
                pad[0:CONV_PAD, :] = tail[:, c0:c1]
                pad[CONV_PAD:CONV_PAD + tm, :] = r
                first = CONV_PAD - (M2_CONV - 1)
                acc = cb_ref[:, c0:c1] + cw_ref[0:1, c0:c1] * pad[first:first + tm, :]
                for k in range(1, M2_CONV):
                    acc = acc + cw_ref[k:k + 1, c0:c1] * pad[first + k:first + k + tm, :]
                tail[:, c0:c1] = pad[tm:tm + CONV_PAD, :]
                r = _silu(acc)
            o_ref[:, c0:c1] = r.astype(o_ref.dtype)


def _rms_proj(h, g, weights, out_dtypes, tm, conv=None, seq_len=None):
    t, d = h.shape
    col_chunk = 256 if conv is not None else 512
    fixed = lambda i: (0, 0)
    in_specs = [pl.BlockSpec((tm, d), lambda i: (i, 0)), pl.BlockSpec((1, d), fixed)]
    in_specs += [pl.BlockSpec(w.shape, fixed) for w in weights]
    args = [h, g.reshape(1, d), *weights]
    scratch, conv_idx, tiles_per_seq = [], None, None
    if conv is not None:
        conv_idx, cw, cb = conv
        n = weights[conv_idx].shape[1]
        in_specs += [pl.BlockSpec(cw.shape, fixed), pl.BlockSpec((1, n), fixed)]
        args += [cw, cb.reshape(1, n)]
        scratch = [pltpu.VMEM((tm + CONV_PAD, col_chunk), F32), pltpu.VMEM((CONV_PAD, n), F32)]
        tiles_per_seq = seq_len // tm
    out_specs = [pl.BlockSpec((tm, w.shape[1]), lambda i: (i, 0)) for w in weights]
    out_shape = [jax.ShapeDtypeStruct((t, w.shape[1]), dt) for w, dt in zip(weights, out_dtypes)]
    return pl.pallas_call(
        functools.partial(_rms_proj_kernel, n_out=len(weights), col_chunk=col_chunk, conv_idx=conv_idx,
                          tiles_per_seq=tiles_per_seq),
        grid=(t // tm,), in_specs=in_specs, out_specs=out_specs, out_shape=out_shape, scratch_shapes=scratch,
        compiler_params=_cparams(("arbitrary",)), name="rms_proj",
    )(*args)


def _out_proj_kernel(h_ref, *refs, n_in, pre):
    x_refs, w_refs = refs[:n_in], refs[n_in:2 * n_in]
    extra = refs[2 * n_in:-1]
    o_ref = refs[-1]
    acc = h_ref[...]
    for i, (x_ref, w_ref) in enumerate(zip(x_refs, w_refs)):
        if i == 0 and pre == "m2":
            z_ref, ng_ref = extra
            for g in range(M2_GROUPS):
                gs = slice(g * M2_GW, (g + 1) * M2_GW)
                gated = x_ref[:, gs].astype(F32) * _silu(z_ref[:, gs].astype(F32))
                xn = gated * lax.rsqrt(jnp.mean(gated * gated, axis=-1, keepdims=True) + RMS_EPS) * ng_ref[:, gs]
                acc = acc + jnp.dot(xn.astype(BF16), w_ref[gs, :], preferred_element_type=F32)
            continue
        x = x_ref[...]
        if i == 0 and pre == "glu":
            za_ref, gw_ref, gb_ref = extra
            gate = jnp.dot(x, gw_ref[...], preferred_element_type=F32) + gb_ref[...]
            x = (x.astype(F32) * jax.nn.sigmoid(gate) * _silu(za_ref[...].astype(F32))).astype(BF16)
        acc = acc + jnp.dot(x, w_ref[...], preferred_element_type=F32)
    o_ref[...] = acc


def _out_proj(h, xs, ws, tm, glu_args=None, m2_args=None):
    t, d = h.shape
    row = lambda i: (i, 0)
    fixed = lambda i: (0, 0)
    in_specs = [pl.BlockSpec((tm, d), row)]
    in_specs += [pl.BlockSpec((tm, x.shape[1]), row) for x in xs]
    in_specs += [pl.BlockSpec(w.shape, fixed) for w in ws]
    args = [h, *xs, *ws]
    if glu_args is not None:
        za, gw, gb = glu_args
        in_specs += [pl.BlockSpec((tm, za.shape[1]), row), pl.BlockSpec(gw.shape, fixed),
                     pl.BlockSpec(gb.shape, fixed)]
        args += [za, gw, gb]
    if m2_args is not None:
        z, ng = m2_args
        in_specs += [pl.BlockSpec((tm, z.shape[1]), row), pl.BlockSpec(ng.shape, fixed)]
        args += [z, ng]
    pre = "glu" if glu_args is not None else "m2" if m2_args is not None else None
    return pl.pallas_call(
        functools.partial(_out_proj_kernel, n_in=len(xs), pre=pre),
        grid=(t // tm,), in_specs=in_specs, out_specs=pl.BlockSpec((tm, d), row),
        out_shape=jax.ShapeDtypeStruct((t, d), F32),
        compiler_params=_cparams(("parallel",)), name="out_proj",
    )(*args)


def _mem_kv_kernel(mem_ref, g_ref, w_ref, kg_ref, k_ref, v_ref):
    x = mem_ref[0]
    xn = (x * lax.rsqrt(jnp.mean(x * x, axis=-1, keepdims=True) + RMS_EPS) * g_ref[0]).astype(BF16)
    kv = jnp.dot(xn, w_ref[0], preferred_element_type=F32)
    for h in range(MEM_HEADS):
        kh = kv[:, h * MEM_HD:(h + 1) * MEM_HD]
        kn = kh * lax.rsqrt(jnp.mean(kh * kh, axis=-1, keepdims=True) + RMS_EPS) * kg_ref[0]
        k_ref[0, 0, :, h * MEM_HD:(h + 1) * MEM_HD] = kn.astype(BF16)
    v_ref[0, 0] = kv[:, MEM_WIDTH:].astype(BF16)


def _mem_kv(mem, mem_norm_g, w_kv, k_norm_g):
    b = mem.shape[0]
    out = jax.ShapeDtypeStruct((DEPTH, b, N_MEM, MEM_WIDTH), BF16)
    return pl.pallas_call(
        _mem_kv_kernel, grid=(DEPTH, b),
        in_specs=[pl.BlockSpec((1, N_MEM, D_MODEL), lambda l, i: (i, 0, 0)),
                  pl.BlockSpec((1, 1, D_MODEL), lambda l, i: (l, 0, 0)),
                  pl.BlockSpec((1, D_MODEL, 2 * MEM_WIDTH), lambda l, i: (l, 0, 0)),
                  pl.BlockSpec((1, 1, MEM_HD), lambda l, i: (l, 0, 0))],
        out_specs=[pl.BlockSpec((1, 1, N_MEM, MEM_WIDTH), lambda l, i: (l, i, 0, 0))] * 2,
        out_shape=[out, out], compiler_params=_cparams(("arbitrary", "arbitrary")), name="mem_kv",
    )(mem, mem_norm_g.reshape(DEPTH, 1, D_MODEL), w_kv.astype(BF16), k_norm_g.reshape(DEPTH, 1, MEM_HD))


def _mem_attn_kernel(q_ref, z_ref, k_ref, v_ref, qg_ref, o_ref):
    scale = 1.0 / math.sqrt(MEM_HD)
    for h in range(MEM_HEADS):
        sl = slice(h * MEM_HD, (h + 1) * MEM_HD)
        q = q_ref[0, :, sl].astype(F32)
        qn = (q * lax.rsqrt(jnp.mean(q * q, axis=-1, keepdims=True) + RMS_EPS) * (qg_ref[...] * scale)).astype(BF16)
        s = lax.dot_general(qn, k_ref[0, 0, :, sl], (((1,), (1,)), ((), ())), preferred_element_type=F32)
        p = jnp.exp(s - jnp.max(s, axis=-1, keepdims=True))
        l = jnp.sum(p, axis=-1, keepdims=True)
        o = jnp.dot(p.astype(BF16), v_ref[0, 0, :, sl], preferred_element_type=F32) / l
        o_ref[0, :, sl] = (o * _silu(z_ref[0, :, sl].astype(F32))).astype(BF16)


def _mem_attn(q, z, k_all, v_all, qg, layer, tq):
    b, s, _ = q.shape
    tok = pl.BlockSpec((1, tq, MEM_WIDTH), lambda i, j: (i, j, 0))
    bank = pl.BlockSpec((1, 1, N_MEM, MEM_WIDTH), lambda i, j: (layer, i, 0, 0))
    return pl.pallas_call(
        _mem_attn_kernel, grid=(b, s // tq),
        in_specs=[tok, tok, bank, bank, pl.BlockSpec((1, MEM_HD), lambda i, j: (0, 0))],
        out_specs=tok, out_shape=jax.ShapeDtypeStruct((b, s, MEM_WIDTH), BF16),
        compiler_params=_cparams(("parallel", "parallel")), name="mem_attn",
    )(q, z, k_all, v_all, qg.reshape(1, MEM_HD))


def _rope_table_kernel(pos_ref, inv_ref, cos_ref, sin_ref):
    ang = pos_ref[0].astype(F32) * inv_ref[...]
    cos_ref[0] = jnp.cos(ang)
    sin_ref[0] = jnp.sin(ang)


def _rope_tables(positions):
    b, s = positions.shape
    inv = ROPE_BASE ** (-jnp.arange(ROPE_HALF, dtype=F32) / ROPE_HALF)
    out = jax.ShapeDtypeStruct((b, ROPE_HALF, s), F32)
    cos, sin = pl.pallas_call(
        _rope_table_kernel, grid=(b,),
        in_specs=[pl.BlockSpec((1, 1, s), lambda i: (i, 0, 0)), pl.BlockSpec((ROPE_HALF, 1), lambda i: (0, 0))],
        out_specs=[pl.BlockSpec((1, ROPE_HALF, s), lambda i: (i, 0, 0))] * 2,
        out_shape=[out, out], compiler_params=_cparams(("parallel",)), name="rope_tables",
    )(positions.reshape(b, 1, s), inv.reshape(ROPE_HALF, 1))
    tail = LANES - MLA_QK
    cos_q = jnp.concatenate([jnp.ones((b, MLA_NOPE, s), F32), cos, cos, jnp.ones((b, tail, s), F32)], 1)
    sin_q = jnp.concatenate([jnp.zeros((b, MLA_NOPE, s), F32), -sin, sin, jnp.zeros((b, tail, s), F32)], 1)
    cos_k = jnp.swapaxes(cos_q, 1, 2).reshape(b * s, LANES)
    sin_k = jnp.swapaxes(sin_q, 1, 2).reshape(b * s, LANES)
    return cos_q, sin_q, cos_k, sin_k


def _head_norm_rope(x, gain, cos, sin, lane):
    xn = x * lax.rsqrt(jnp.sum(x * x, axis=-1, keepdims=True) * (1.0 / MLA_QK) + RMS_EPS) * gain
    partner = jnp.where(lane < MLA_NOPE + ROPE_HALF,
                        pltpu.roll(xn, LANES - ROPE_HALF, 1), pltpu.roll(xn, ROPE_HALF, 1))
    return xn * cos + partner * sin


def _mla_prep_kernel(cq_ref, ckv_ref, kr_ref, cosq_ref, sinq_ref, cosk_ref, sink_ref, gqa_ref, wqt_ref, gkva_ref,
                     wk_ref, wvt_ref, gq_ref, gk_ref, qt_ref, k_ref, vt_ref):
    cq = cq_ref[...]
    cqn = cq * lax.rsqrt(jnp.mean(cq * cq, axis=-1, keepdims=True) + RMS_EPS) * gqa_ref[...]
    ckv = ckv_ref[...]
    ckvn = ckv * lax.rsqrt(jnp.mean(ckv * ckv, axis=-1, keepdims=True) + RMS_EPS) * gkva_ref[...]
    cqn_t = cqn.T.astype(BF16)
    ckvn_t = ckvn.T.astype(BF16)
    ckvn = ckvn.astype(BF16)
    kr = kr_ref[...]
    cosq, sinq, cosk, sink = cosq_ref[0], sinq_ref[0], cosk_ref[...], sink_ref[...]
    lane = lax.broadcasted_iota(jnp.int32, cosk.shape, 1)
    row = lax.broadcasted_iota(jnp.int32, cosq.shape, 0)
    qscale = math.log2(math.e) / math.sqrt(MLA_QK)
    tm = cq.shape[0]
    ones_tile = jnp.where(lax.broadcasted_iota(jnp.int32, (MLA_VA - MLA_V, tm), 0) == 0, 1.0, 0.0).astype(BF16)
    for h in range(MLA_HEADS):
        sl = slice(h * LANES, (h + 1) * LANES)
        qt = jnp.dot(wqt_ref[sl, :], cqn_t, preferred_element_type=F32)
        qn = qt * lax.rsqrt(jnp.sum(qt * qt, axis=0, keepdims=True) * (1.0 / MLA_QK) + RMS_EPS) * gq_ref[...]
        partner = jnp.where(row < MLA_NOPE + ROPE_HALF,
                            pltpu.roll(qn, LANES - ROPE_HALF, 0), pltpu.roll(qn, ROPE_HALF, 0))
        qt_ref[0, 0, sl, :] = ((qn * cosq + partner * sinq) * qscale).astype(BF16)
        kh = jnp.dot(ckvn, wk_ref[:, sl], preferred_element_type=F32) + kr
        k_ref[:, sl] = _head_norm_rope(kh, gk_ref[...], cosk, sink, lane).astype(BF16)
        vs = slice(h * MLA_V, (h + 1) * MLA_V)
        vt_ref[0, 0, h * MLA_VA:h * MLA_VA + MLA_V, :] = jnp.dot(wvt_ref[vs, :], ckvn_t,
                                                                 preferred_element_type=F32).astype(BF16)
        vt_ref[0, 0, h * MLA_VA + MLA_V:(h + 1) * MLA_VA, :] = ones_tile


def _pad_heads(w, n_heads, width, offset=0):
    k = w.shape[0]
    w = w.reshape(k, n_heads, width)
    w = jnp.pad(w, ((0, 0), (0, 0), (offset, LANES - width - offset)))
    return w.reshape(k, n_heads * LANES)


def _mla_prep(cq, ckv, kr, rope_t, gqa, w_uq, gkva, w_ukv, gq, gk, b, s, tm):
    cos_q, sin_q, cos_k, sin_k = rope_t
    nt = s // tm
    wqt = _pad_heads(w_uq, MLA_HEADS, MLA_QK).T.astype(BF16)
    w_ukv = w_ukv.reshape(MLA_KV_LORA, MLA_HEADS, MLA_NOPE + MLA_V)
    wk = _pad_heads(w_ukv[:, :, :MLA_NOPE].reshape(MLA_KV_LORA, -1), MLA_HEADS, MLA_NOPE).astype(BF16)
    wvt = w_ukv[:, :, MLA_NOPE:].reshape(MLA_KV_LORA, MLA_WIDTH).T.astype(BF16)
    pad_gain = lambda g: jnp.pad(g, (0, LANES - MLA_QK))
    row = lambda i, j: (i * nt + j, 0)
    fixed = lambda i, j: (0, 0)
    hw = MLA_HEADS * LANES
    return pl.pallas_call(
        _mla_prep_kernel, grid=(b, nt),
        in_specs=[pl.BlockSpec((tm, MLA_Q_LORA), row), pl.BlockSpec((tm, MLA_KV_LORA), row),
                  pl.BlockSpec((tm, LANES), row),
                  pl.BlockSpec((1, LANES, tm), lambda i, j: (i, 0, j)), pl.BlockSpec((1, LANES, tm), lambda i, j: (i, 0, j)),
                  pl.BlockSpec((tm, LANES), row), pl.BlockSpec((tm, LANES), row),
                  pl.BlockSpec((1, MLA_Q_LORA), fixed), pl.BlockSpec((hw, MLA_Q_LORA), fixed),
                  pl.BlockSpec((1, MLA_KV_LORA), fixed), pl.BlockSpec((MLA_KV_LORA, hw), fixed),
                  pl.BlockSpec((MLA_WIDTH, MLA_KV_LORA), fixed),
                  pl.BlockSpec((LANES, 1), fixed), pl.BlockSpec((1, LANES), fixed)],
        out_specs=[pl.BlockSpec((1, 1, hw, tm), lambda i, j: (i, j, 0, 0)), pl.BlockSpec((tm, hw), row),
                   pl.BlockSpec((1, 1, MLA_HEADS * MLA_VA, tm), lambda i, j: (i, j, 0, 0))],
        out_shape=[jax.ShapeDtypeStruct((b, nt, hw, tm), BF16), jax.ShapeDtypeStruct((b * s, hw), BF16),
                   jax.ShapeDtypeStruct((b, nt, MLA_HEADS * MLA_VA, tm), BF16)],
        compiler_params=_cparams(("parallel", "parallel")), name="mla_prep",
    )(cq, ckv, kr, cos_q, sin_q, cos_k, sin_k, gqa.reshape(1, -1), wqt, gkva.reshape(1, -1), wk, wvt,
      pad_gain(gq).reshape(LANES, 1), pad_gain(gk).reshape(1, LANES))


def _mla_attn_kernel(q_ref, k_ref, v_ref, z_ref, o_ref, m_sc, acc_sc, *, tq):
    qi = pl.program_id(2)
    m_sc[...] = jnp.full(m_sc.shape, -jnp.inf, F32)
    acc_sc[...] = jnp.zeros(acc_sc.shape, F32)

    def block(ki, masked):
        start = pl.multiple_of(ki * tq, tq)
        for h in range(2):
            s = jnp.dot(k_ref[0, pl.ds(start, tq), h * LANES:(h + 1) * LANES], q_ref[0, 0, h * LANES:(h + 1) * LANES, :],
                        preferred_element_type=F32)
            if masked:
                kc = lax.broadcasted_iota(jnp.int32, s.shape, 0) // CHUNK
                qc = lax.broadcasted_iota(jnp.int32, s.shape, 1) // CHUNK
                s = jnp.where(kc <= qc, s, -jnp.inf)
            m_prev = m_sc[h]
            m_new = jnp.maximum(m_prev, jnp.max(s, axis=0, keepdims=True))
            alpha = jnp.exp2(m_prev - m_new)
            p = jnp.exp2(s - m_new)
            acc_sc[h] = alpha * acc_sc[h] + jnp.dot(v_ref[0, ki, h * MLA_VA:(h + 1) * MLA_VA, :], p.astype(BF16),
                                                    preferred_element_type=F32)
            m_sc[h] = m_new

    def body(ki, carry):
        block(ki, masked=False)
        return carry

    lax.fori_loop(0, qi, body, 0)
    block(qi, masked=True)
    o_t = jnp.concatenate([acc_sc[h, :MLA_V, :] / acc_sc[h, MLA_V:MLA_V + 1, :] for h in range(2)], axis=0)
    o_ref[0] = (o_t.T * _silu(z_ref[0].astype(F32))).astype(BF16)


def _mla_attn(qt, k, vt, z, tq):
    b, s, _ = k.shape
    nt = s // tq
    pairs = MLA_HEADS // 2
    return pl.pallas_call(
        functools.partial(_mla_attn_kernel, tq=tq), grid=(b, pairs, nt),
        in_specs=[pl.BlockSpec((1, 1, 2 * LANES, tq), lambda i, p, j: (i, j, p, 0)),
                  pl.BlockSpec((1, s, 2 * LANES), lambda i, p, j: (i, 0, p)),
                  pl.BlockSpec((1, nt, 2 * MLA_VA, tq), lambda i, p, j: (i, 0, p, 0)),
                  pl.BlockSpec((1, tq, LANES), lambda i, p, j: (i, j, p))],
        out_specs=pl.BlockSpec((1, tq, LANES), lambda i, p, j: (i, j, p)),
        out_shape=jax.ShapeDtypeStruct((b, s, MLA_WIDTH), BF16),
        scratch_shapes=[pltpu.VMEM((2, 1, tq), F32), pltpu.VMEM((2, MLA_VA, tq), F32)],
        compiler_params=_cparams(("parallel", "parallel", "arbitrary")), name="mla_attn",
    )(qt, k, vt, z)


def _s5_tables(a_re, a_im, log_dt, b_re, b_im, c_re, c_im, n_steps):
    hp = lax.Precision.HIGHEST
    dt = jnp.exp(log_dt)[:, None]
    mag = jnp.exp(a_re * dt)
    ab_re, ab_im = mag * jnp.cos(a_im * dt), mag * jnp.sin(a_im * dt)
    den = a_re * a_re + a_im * a_im
    n_re, n_im = ab_re - 1.0, ab_im
    f_re = (n_re * a_re + n_im * a_im) / den
    f_im = (n_im * a_re - n_re * a_im) / den
    bb_re = f_re[..., None] * b_re - f_im[..., None] * b_im
    bb_im = f_re[..., None] * b_im + f_im[..., None] * b_re

    def lam_pow(k):
        k = k.astype(F32)[:, None, None]
        m = jnp.exp(k * a_re * dt)
        return m * jnp.cos(k * a_im * dt), m * jnp.sin(k * a_im * dt)

    L, C, P, G = S5_L, S5_GROUP, S5_STATE, S5_GROUPS
    lr, li = lam_pow(jnp.arange(L + 1))
    w_re = lr[..., None] * bb_re[None] - li[..., None] * bb_im[None]
    w_im = lr[..., None] * bb_im[None] + li[..., None] * bb_re[None]
    kk = (jnp.einsum("gdp,kgpc->kgdc", c_re, w_re[:L], precision=hp)
          - jnp.einsum("gdp,kgpc->kgdc", c_im, w_im[:L], precision=hp))
    lag = jnp.arange(L)[None, :] - jnp.arange(L)[:, None]
    kfull = jnp.where((lag >= 0)[:, :, None, None, None], kk[jnp.clip(lag, 0, L - 1)], 0.0)
    mt = jnp.transpose(kfull, (2, 0, 4, 1, 3)).reshape(G, L * C, L * C)
    e_re = jnp.transpose(w_re[:L][::-1], (1, 0, 3, 2)).reshape(G, L * C, P)
    e_im = jnp.transpose(w_im[:L][::-1], (1, 0, 3, 2)).reshape(G, L * C, P)
    w1 = jnp.concatenate([mt, e_re, e_im, -e_im, e_re], axis=-1)
    g_re = c_re[None] * lr[1:, :, None, :] - c_im[None] * li[1:, :, None, :]
    g_im = c_re[None] * li[1:, :, None, :] + c_im[None] * lr[1:, :, None, :]
    f_mat = jnp.concatenate([jnp.transpose(g_re, (1, 3, 0, 2)).reshape(G, P, L * C),
                             -jnp.transpose(g_im, (1, 3, 0, 2)).reshape(G, P, L * C)], axis=1)
    sr, si = lam_pow(L * (2 ** jnp.arange(n_steps)))
    la = jnp.transpose(jnp.concatenate([sr, sr], -1), (1, 0, 2))[:, :, None, :]
    lb = jnp.transpose(jnp.concatenate([si, si], -1), (1, 0, 2))[:, :, None, :]
    return w1.astype(BF16), f_mat.astype(BF16), la, lb


def _s5_kernel(u_ref, w1_ref, f_ref, la_ref, lb_ref, d_ref, y_ref, *, gb, n_steps):
    nc = u_ref.shape[2]
    row = lax.broadcasted_iota(jnp.int32, (nc, 2 * S5_STATE), 0)

    def shift(t, d):
        return jnp.where(row >= d, pltpu.roll(t, d, 0), 0.0)

    for g in range(gb):
        u = u_ref[0, g]
        r = jnp.dot(u, w1_ref[g], preferred_element_type=F32)
        y = r[:, :S5_FLAT]
        x = r[:, S5_FLAT:S5_FLAT + 2 * S5_STATE]
        xs = r[:, S5_FLAT + 2 * S5_STATE:]
        for s in range(n_steps):
            a, b = la_ref[g, s], lb_ref[g, s]
            tx, txs = a * x + b * xs, a * xs - b * x
            x, xs = x + shift(tx, 1 << s), xs + shift(txs, 1 << s)
        h_in = shift(x, 1).astype(BF16)
        y = y + jnp.dot(h_in, f_ref[g], preferred_element_type=F32) + d_ref[g] * u.astype(F32)
        y_ref[0, g] = jax.nn.gelu(y, approximate=True).astype(BF16)


def _s5(u, tables, d, gb):
    w1, f_mat, la, lb = tables
    b, s, _ = u.shape
    nc = s // S5_L
    n_steps = la.shape[1]
    G = S5_GROUPS
    half = S5_GROUP // 2
    u32 = lax.bitcast_convert_type(u.reshape(b, nc, S5_L, G, half, 2), jnp.uint32)
    uf = lax.bitcast_convert_type(u32.transpose(0, 3, 1, 2, 4), BF16).reshape(b, G, nc, S5_FLAT)
    d_flat = jnp.tile(d.reshape(G, 1, S5_GROUP), (1, 1, S5_L))
    grp = lambda i, j: (j, 0, 0)
    y = pl.pallas_call(
        functools.partial(_s5_kernel, gb=gb, n_steps=n_steps), grid=(b, G // gb),
        in_specs=[pl.BlockSpec((1, gb, nc, S5_FLAT), lambda i, j: (i, j, 0, 0)),
                  pl.BlockSpec((gb, S5_FLAT, 2 * S5_FLAT), grp),
                  pl.BlockSpec((gb, 2 * S5_STATE, S5_FLAT), grp),
                  pl.BlockSpec((gb, n_steps, 1, 2 * S5_STATE), lambda i, j: (j, 0, 0, 0)),
                  pl.BlockSpec((gb, n_steps, 1, 2 * S5_STATE), lambda i, j: (j, 0, 0, 0)),
                  pl.BlockSpec((gb, 1, S5_FLAT), grp)],
        out_specs=pl.BlockSpec((1, gb, nc, S5_FLAT), lambda i, j: (i, j, 0, 0)),
        out_shape=jax.ShapeDtypeStruct((b, G, nc, S5_FLAT), BF16),
        compiler_params=_cparams(("parallel", "parallel")), name="s5",
    )(uf, w1, f_mat, la, lb, d_flat)
    y32 = lax.bitcast_convert_type(y.reshape(b, G, nc, S5_L, half, 2), jnp.uint32)
    return lax.bitcast_convert_type(y32.transpose(0, 2, 3, 1, 4), BF16).reshape(b, s, S5_WIDTH)


def _split_dot(x, e):
    hi = x.astype(BF16)
    lo = (x - hi.astype(F32)).astype(BF16)
    return jnp.dot(hi, e, preferred_element_type=F32) + jnp.dot(lo, e, preferred_element_type=F32)


def _ssd_kernel(xc_ref, dt_ref, dtb_ref, alog_ref, dexp_ref, e_ref, y_ref, state):
    L, PAD = M2_CHUNK, 8

    @pl.when(pl.program_id(1) == 0)
    def _():
        state[...] = jnp.zeros_like(state)

    xc = xc_ref[0].astype(F32)
    xs = xc[:, :M2_INNER]
    nbc = M2_GROUPS * M2_STATE
    bm = xc[:, M2_INNER:M2_INNER + nbc]
    cm = xc[:, M2_INNER + nbc:]

    dt = jax.nn.softplus(dt_ref[0] + dtb_ref[...])
    da = dt * (-jnp.exp(alog_ref[...]))
    ti = lax.broadcasted_iota(jnp.int32, (L, L), 0)
    si = lax.broadcasted_iota(jnp.int32, (L, L), 1)
    causal = si <= ti
    tril =jnp.where(causal, 1.0, 0.0).astype(BF16)
    hi = da.astype(BF16)
    r1 = da - hi.astype(F32)
    mid = r1.astype(BF16)
    lo = (r1 - mid.astype(F32)).astype(BF16)
    cs = (jnp.dot(tril, hi, preferred_element_type=F32) + jnp.dot(tril, mid, preferred_element_type=F32)
          + jnp.dot(tril, lo, preferred_element_type=F32))
    cs2 = cs * math.log2(math.e)
    cs2_t = cs2.T
    cs_end = cs[L - 1:L, :]
    stack = jnp.concatenate([dt, dt * jnp.exp(cs_end - cs), jnp.exp(cs)], axis=0).astype(BF16)
    fac = jnp.dot(stack, e_ref[...], preferred_element_type=F32)
    dt_e, dw_e, ecs_e = fac[:L], fac[L:2 * L], fac[2 * L:]
    dend_e = _split_dot(jnp.broadcast_to(jnp.exp(cs_end), (PAD, LANES)), e_ref[...])[0:1]
    x_dt = (xs * dt_e).astype(BF16)
    x_w = (xs * dw_e).astype(BF16)

    lane = lax.broadcasted_iota(jnp.int32, (L, LANES), 1)
    zero = jnp.zeros((L, LANES), BF16)
    for g in range(M2_GROUPS):
        gs = slice(g * M2_GW, (g + 1) * M2_GW)
        b_g = bm[:, g * M2_STATE:(g + 1) * M2_STATE]
        c_g = cm[:, g * M2_STATE:(g + 1) * M2_STATE].astype(BF16)
        cb = lax.dot_general(c_g, b_g.astype(BF16), (((1,), (1,)), ((), ())), preferred_element_type=F32)
        s_old = state[g]
        y_off = jnp.dot(c_g, s_old.astype(BF16), preferred_element_type=F32) * ecs_e[:, gs]
        state[g] = s_old * dend_e[:, gs] + jnp.dot(b_g.T.astype(BF16), x_w[:, gs], preferred_element_type=F32)
        for pr in range(M2_GW // LANES):
            ps = slice(g * M2_GW + pr * LANES, g * M2_GW + (pr + 1) * LANES)
            xp = x_dt[:, ps]
            y_pair = None
            for hh in range(2):
                h = (g * M2_GW + pr * LANES) // M2_HEADDIM + hh
                seg = jnp.exp2(jnp.where(causal, cs2[:, h:h + 1] - cs2_t[h:h + 1, :], -jnp.inf))
                att = (cb * seg).astype(BF16)
                x_h = jnp.where((lane < M2_HEADDIM) == (hh == 0), xp, zero)
                part = jnp.dot(att, x_h, preferred_element_type=F32)
                y_pair = part if y_pair is None else y_pair + part
            y = y_pair + y_off[:, pr * LANES:(pr + 1) * LANES] + dexp_ref[:, ps] * xs[:, ps]
            y_ref[0, :, ps] = y.astype(y_ref.dtype)


def _ssd(xc, dt_raw, dt_bias, a_log, d):
    b, s, _ = xc.shape
    pad_row = lambda v: jnp.pad(v, (0, LANES - M2_HEADS)).reshape(1, LANES)
    expand = (jnp.arange(LANES)[:, None] == (jnp.arange(M2_INNER) // M2_HEADDIM)[None, :]).astype(BF16)
    tok = lambda w: pl.BlockSpec((1, M2_CHUNK, w), lambda i, j: (i, j, 0))
    fixed = lambda shape: pl.BlockSpec(shape, lambda i, j: (0, 0))
    return pl.pallas_call(
        _ssd_kernel, grid=(b, s // M2_CHUNK),
        in_specs=[tok(M2_CONV_DIM), tok(LANES), fixed((1, LANES)), fixed((1, LANES)), fixed((1, M2_INNER)),
                  fixed((LANES, M2_INNER))],
        out_specs=tok(M2_INNER), out_shape=jax.ShapeDtypeStruct((b, s, M2_INNER), BF16),
        scratch_shapes=[pltpu.VMEM((M2_GROUPS, M2_STATE, M2_GW), F32)],
        compiler_params=_cparams(("parallel", "arbitrary")), name="ssd",
    )(xc, dt_raw, pad_row(dt_bias), pad_row(a_log), jnp.repeat(d, M2_HEADDIM).reshape(1, M2_INNER), expand)


def _cols(w, sizes):
    idx, out = 0, []
    for n in sizes:
        out.append(w[:, idx:idx + n])
        idx += n
    return out


def _even_layer(h, b, s, i, layer, p, mem_k, mem_v, rope_t, tm, tq):
    w_u, w_za, w_cq, w_ckv, w_kr, w_zb, w_qm, w_zm = _cols(
        p["ev_w_in"][i], (S5_WIDTH, S5_WIDTH, MLA_Q_LORA, MLA_KV_LORA, MLA_ROPE, MLA_WIDTH, MEM_WIDTH, MEM_WIDTH))
    w_kr = jnp.pad(w_kr, ((0, 0), (MLA_NOPE, LANES - MLA_QK)))
    weights = [w.astype(BF16) for w in (w_u, w_za, w_zb, w_qm, w_zm, w_cq, w_ckv, w_kr)]
    u, za, zb, qm, zm, cq, ckv, kr = _rms_proj(h, p["norm_g"][layer], weights, [BF16] * 5 + [F32] * 3, tm)

    n_steps = max(1, (s // S5_L - 1).bit_length())
    tables = _s5_tables(p["s5_a_re"][i], p["s5_a_im"][i], p["s5_log_dt"][i], p["s5_b_re"][i], p["s5_b_im"][i],
                        p["s5_c_re"][i], p["s5_c_im"][i], n_steps)
    y_s5 = _s5(u.reshape(b, s, S5_WIDTH), tables, p["s5_d"][i], gb=8).reshape(b * s, S5_WIDTH)

    qt, k, vt = _mla_prep(cq, ckv, kr, rope_t, p["mla_q_a_norm_g"][i], p["mla_w_uq"][i], p["mla_kv_a_norm_g"][i],
                          p["mla_w_ukv"][i], p["mla_q_norm_g"][i], p["mla_k_norm_g"][i], b, s, tq)
    y_b = _mla_attn(qt, k.reshape(b, s, MLA_HEADS * LANES), vt, zb.reshape(b, s, MLA_WIDTH),
                    tq).reshape(b * s, MLA_WIDTH)

    y_m = _mem_attn(qm.reshape(b, s, MEM_WIDTH), zm.reshape(b, s, MEM_WIDTH), mem_k, mem_v,
                    p["mem_q_norm_g"][layer], layer, 2 * tm if s % (2 * tm) == 0 else tm).reshape(b * s, MEM_WIDTH)

    w_out = p["ev_w_out"][i].astype(BF16)
    ws = [w_out[:S5_WIDTH], w_out[S5_WIDTH:S5_WIDTH + MLA_WIDTH], w_out[S5_WIDTH + MLA_WIDTH:]]
    glu = (za, p["s5_glu_w"][i].astype(BF16), p["s5_glu_b"][i].reshape(1, S5_WIDTH))
    return _out_proj(h, [y_s5, y_b, y_m], ws, tm, glu)


def _odd_layer(h, b, s, i, layer, p, mem_k, mem_v, tm):
    w_z, w_xbc, w_dt, w_qm, w_zm = _cols(p["od_w_in"][i], (M2_INNER, M2_CONV_DIM, M2_HEADS, MEM_WIDTH, MEM_WIDTH))
    w_dt = jnp.pad(w_dt, ((0, 0), (0, LANES - M2_HEADS)))
    weights = [w.astype(BF16) for w in (w_z, w_xbc, w_qm, w_zm, w_dt)]
    z, xc, qm, zm, dt_raw = _rms_proj(h, p["norm_g"][layer], weights, [BF16] * 4 + [F32], tm,
                                      conv=(1, p["m2_conv_w"][i], p["m2_conv_b"][i]), seq_len=s)
    y_c = _ssd(xc.reshape(b, s, M2_CONV_DIM), dt_raw.reshape(b, s, LANES), p["m2_dt_bias"][i], p["m2_a_log"][i],
               p["m2_d"][i]).reshape(b * s, M2_INNER)
    y_m = _mem_attn(qm.reshape(b, s, MEM_WIDTH), zm.reshape(b, s, MEM_WIDTH), mem_k, mem_v,
                    p["mem_q_norm_g"][layer], layer, 2 * tm if s % (2 * tm) == 0 else tm).reshape(b * s, MEM_WIDTH)
    w_out = p["od_w_out"][i].astype(BF16)
    return _out_proj(h, [y_c, y_m], [w_out[:M2_INNER], w_out[M2_INNER:]], tm,
                     m2_args=(z, p["m2_norm_g"][i].reshape(1, M2_INNER)))


def _token_tile(s):
    return 512 if s % 512 == 0 else s


def kernel(x, mem, positions, norm_g, mem_norm_g, mem_w_kv, mem_q_norm_g, mem_k_norm_g, ev_w_in, ev_w_out, s5_a_re, s5_a_im, s5_log_dt, s5_b_re, s5_b_im, s5_c_re, s5_c_im, s5_d, s5_glu_w, s5_glu_b, mla_q_a_norm_g, mla_w_uq, mla_kv_a_norm_g, mla_w_ukv, mla_q_norm_g, mla_k_norm_g, od_w_in, od_w_out, m2_conv_w, m2_conv_b, m2_dt_bias, m2_a_log, m2_d, m2_norm_g):
    p = dict(norm_g=norm_g, mem_q_norm_g=mem_q_norm_g, ev_w_in=ev_w_in, ev_w_out=ev_w_out,
             s5_a_re=s5_a_re, s5_a_im=s5_a_im, s5_log_dt=s5_log_dt, s5_b_re=s5_b_re, s5_b_im=s5_b_im,
             s5_c_re=s5_c_re, s5_c_im=s5_c_im, s5_d=s5_d, s5_glu_w=s5_glu_w, s5_glu_b=s5_glu_b,
             mla_q_a_norm_g=mla_q_a_norm_g, mla_w_uq=mla_w_uq, mla_kv_a_norm_g=mla_kv_a_norm_g,
             mla_w_ukv=mla_w_ukv, mla_q_norm_g=mla_q_norm_g, mla_k_norm_g=mla_k_norm_g,
             od_w_in=od_w_in, od_w_out=od_w_out, m2_conv_w=m2_conv_w, m2_conv_b=m2_conv_b,
             m2_dt_bias=m2_dt_bias, m2_a_log=m2_a_log, m2_d=m2_d, m2_norm_g=m2_norm_g)
    b, s, d = x.shape
    tm = _token_tile(s)
    tq = 1024 if s % 1024 == 0 else s
    mem_k, mem_v = _mem_kv(mem, mem_norm_g, mem_w_kv, mem_k_norm_g)
    rope_t = _rope_tables(positions)
    h = x.reshape(b * s, d)
    for layer in range(DEPTH):
        if layer % 2 == 0:
            h = _even_layer(h, b, s, layer // 2, layer, p, mem_k, mem_v, rope_t, tm, tq)
        else:
            h = _odd_layer(h, b, s, layer // 2, layer, p, mem_k, mem_v, tm)
    return h.reshape(b, s, d)
```

```python
import functools
import math

import jax
import jax.numpy as jnp
from jax import lax
from jax.experimental import pallas as pl
from jax.experimental.pallas import tpu as pltpu

F32 = jnp.float32
BF16 = jnp.bfloat16

D_MODEL = 1024
DEPTH = 4
CHUNK = 64
N_MEM = 256
RMS_EPS = 1e-6

S5_WIDTH = 512
S5_GROUP = 16
S5_GROUPS = S5_WIDTH // S5_GROUP
S5_STATE = 64
S5_L = 16
S5_FLAT = S5_L * S5_GROUP

MLA_HEADS = 8
MLA_NOPE = 64
MLA_ROPE = 32
MLA_QK = MLA_NOPE + MLA_ROPE
MLA_V = 64
MLA_WIDTH = MLA_HEADS * MLA_V
MLA_VA = MLA_V + 16
MLA_Q_LORA = 256
MLA_KV_LORA = 128
ROPE_BASE = 10000.0
ROPE_HALF = MLA_ROPE // 2
LANES = 128

M2_INNER = 2 * D_MODEL
M2_HEADDIM = 64
M2_HEADS = M2_INNER // M2_HEADDIM
M2_GROUPS = 4
M2_STATE = 128
M2_CONV = 4
M2_CHUNK = 128
M2_CONV_DIM = M2_INNER + 2 * M2_GROUPS * M2_STATE
M2_GW = M2_INNER // M2_GROUPS

MEM_HEADS = 4
MEM_HD = 128
MEM_WIDTH = MEM_HEADS * MEM_HD

VMEM_LIMIT = 56 * 1024 * 1024


def _cparams(sem):
    return pltpu.CompilerParams(dimension_semantics=sem, vmem_limit_bytes=VMEM_LIMIT)


def _silu(z):
    h = 0.5 * z
    return h + h * jnp.tanh(h)


CONV_PAD = 8


def _rms_proj_kernel(h_ref, g_ref, *refs, n_out, col_chunk, conv_idx, tiles_per_seq):
    w_refs = refs[:n_out]
    if conv_idx is None:
        o_refs = refs[n_out:]
    else:
        cw_ref, cb_ref = refs[n_out:n_out + 2]
        o_refs = refs[n_out + 2:2 * n_out + 2]
        pad, tail = refs[2 * n_out + 2:]

        @pl.when(pl.program_id(0) % tiles_per_seq == 0)
        def _():
            tail[...] = jnp.zeros_like(tail)

    x = h_ref[...]
    tm = x.shape[0]
    xn = (x * lax.rsqrt(jnp.mean(x * x, axis=-1, keepdims=True) + RMS_EPS) * g_ref[...]).astype(BF16)
    for i, (w_ref, o_ref) in enumerate(zip(w_refs, o_refs)):
        n = w_ref.shape[1]
        for c0 in range(0, n, col_chunk):
            c1 = min(n, c0 + col_chunk)
            r = jnp.dot(xn, w_ref[:, c0:c1], preferred_element_type=F32)
            if i == conv_idx:
                pad[0:CONV_PAD, :] = tail[:, c0:c1]
                pad[CONV_PAD:CONV_PAD + tm, :] = r
                first = CONV_PAD - (M2_CONV - 1)
                acc = cb_ref[:, c0:c1] + cw_ref[0:1, c0:c1] * pad[first:first + tm, :]
                for k in range(1, M2_CONV):
                    acc = acc + cw_ref[k:k + 1, c0:c1] * pad[first + k:first + k + tm, :]
                tail[:, c0:c1] = pad[tm:tm + CONV_PAD, :]
                r = _silu(acc)
            o_ref[:, c0:c1] = r.astype(o_ref.dtype)


def _rms_proj(h, g, weights, out_dtypes, tm, conv=None, seq_len=None):
    t, d = h.shape
    col_chunk = 256 if conv is not None else 512
    fixed = lambda i: (0, 0)
    in_specs = [pl.BlockSpec((tm, d), lambda i: (i, 0)), pl.BlockSpec((1, d), fixed)]
    in_specs += [pl.BlockSpec(w.shape, fixed) for w in weights]
    args = [h, g.reshape(1, d), *weights]
    scratch, conv_idx, tiles_per_seq = [], None, None
    if conv is not None:
        conv_idx, cw, cb = conv
        n = weights[conv_idx].shape[1]
        in_specs += [pl.BlockSpec(cw.shape, fixed), pl.BlockSpec((1, n), fixed)]
        args += [cw, cb.reshape(1, n)]
        scratch = [pltpu.VMEM((tm + CONV_PAD, col_chunk), F32), pltpu.VMEM((CONV_PAD, n), F32)]
        tiles_per_seq = seq_len // tm
    out_specs = [pl.BlockSpec((tm, w.shape[1]), lambda i: (i, 0)) for w in weights]
    out_shape = [jax.ShapeDtypeStruct((t, w.shape[1]), dt) for w, dt in zip(weights, out_dtypes)]
    return pl.pallas_call(
        functools.partial(_rms_proj_kernel, n_out=len(weights), col_chunk=col_chunk, conv_idx=conv_idx,
                          tiles_per_seq=tiles_per_seq),
        grid=(t // tm,), in_specs=in_specs, out_specs=out_specs, out_shape=out_shape, scratch_shapes=scratch,
        compiler_params=_cparams(("arbitrary",)), name="rms_proj",
    )(*args)


def _out_proj_kernel(h_ref, *refs, n_in, pre):
    x_refs, w_refs = refs[:n_in], refs[n_in:2 * n_in]
    extra = refs[2 * n_in:-1]
    o_ref = refs[-1]
    acc = h_ref[...]
    for i, (x_ref, w_ref) in enumerate(zip(x_refs, w_refs)):
        if i == 0 and pre == "m2":
            z_ref, ng_ref = extra
            for g in range(M2_GROUPS):
                gs = slice(g * M2_GW, (g + 1) * M2_GW)
                gated = x_ref[:, gs].astype(F32) * _silu(z_ref[:, gs].astype(F32))
                xn = gated * lax.rsqrt(jnp.mean(gated * gated, axis=-1, keepdims=True) + RMS_EPS) * ng_ref[:, gs]
                acc = acc + jnp.dot(xn.astype(BF16), w_ref[gs, :], preferred_element_type=F32)
            continue
        x = x_ref[...]
        if i == 0 and pre == "glu":
            za_ref, gw_ref, gb_ref = extra
            gate = jnp.dot(x, gw_ref[...], preferred_element_type=F32) + gb_ref[...]
            x = (x.astype(F32) * jax.nn.sigmoid(gate) * _silu(za_ref[...].astype(F32))).astype(BF16)
        acc = acc + jnp.dot(x, w_ref[...], preferred_element_type=F32)
    o_ref[...] = acc


def _out_proj(h, xs, ws, tm, glu_args=None, m2_args=None):
    t, d = h.shape
    row = lambda i: (i, 0)
    fixed = lambda i: (0, 0)
    in_specs = [pl.BlockSpec((tm, d), row)]
    in_specs += [pl.BlockSpec((tm, x.shape[1]), row) for x in xs]
    in_specs += [pl.BlockSpec(w.shape, fixed) for w in ws]
    args = [h, *xs, *ws]
    if glu_args is not None:
        za, gw, gb = glu_args
        in_specs += [pl.BlockSpec((tm, za.shape[1]), row), pl.BlockSpec(gw.shape, fixed),
                     pl.BlockSpec(gb.shape, fixed)]
        args += [za, gw, gb]
    if m2_args is not None:
        z, ng = m2_args
        in_specs += [pl.BlockSpec((tm, z.shape[1]), row), pl.BlockSpec(ng.shape, fixed)]
        args += [z, ng]
    pre = "glu" if glu_args is not None else "m2" if m2_args is not None else None
    return pl.pallas_call(
        functools.partial(_out_proj_kernel, n_in=len(xs), pre=pre),
        grid=(t // tm,), in_specs=in_specs, out_specs=pl.BlockSpec((tm, d), row),
        out_shape=jax.ShapeDtypeStruct((t, d), F32),
        compiler_params=_cparams(("parallel",)), name="out_proj",
    )(*args)


def _mem_kv_kernel(mem_ref, g_ref, w_ref, kg_ref, k_ref, v_ref):
    x = mem_ref[0]
    xn = (x * lax.rsqrt(jnp.mean(x * x, axis=-1, keepdims=True) + RMS_EPS) * g_ref[0]).astype(BF16)
    kv = jnp.dot(xn, w_ref[0], preferred_element_type=F32)
    for h in range(MEM_HEADS):
        kh = kv[:, h * MEM_HD:(h + 1) * MEM_HD]
        kn = kh * lax.rsqrt(jnp.mean(kh * kh, axis=-1, keepdims=True) + RMS_EPS) * kg_ref[0]
        k_ref[0, 0, :, h * MEM_HD:(h + 1) * MEM_HD] = kn.astype(BF16)
    v_ref[0, 0] = kv[:, MEM_WIDTH:].astype(BF16)


def _mem_kv(mem, mem_norm_g, w_kv, k_norm_g):
    b = mem.shape[0]
    out = jax.ShapeDtypeStruct((DEPTH, b, N_MEM, MEM_WIDTH), BF16)
    return pl.pallas_call(
        _mem_kv_kernel, grid=(DEPTH, b),
        in_specs=[pl.BlockSpec((1, N_MEM, D_MODEL), lambda l, i: (i, 0, 0)),
                  pl.BlockSpec((1, 1, D_MODEL), lambda l, i: (l, 0, 0)),
                  pl.BlockSpec((1, D_MODEL, 2 * MEM_WIDTH), lambda l, i: (l, 0, 0)),
                  pl.BlockSpec((1, 1, MEM_HD), lambda l, i: (l, 0, 0))],
        out_specs=[pl.BlockSpec((1, 1, N_MEM, MEM_WIDTH), lambda l, i: (l, i, 0, 0))] * 2,
        out_shape=[out, out], compiler_params=_cparams(("arbitrary", "arbitrary")), name="mem_kv",
    )(mem, mem_norm_g.reshape(DEPTH, 1, D_MODEL), w_kv.astype(BF16), k_norm_g.reshape(DEPTH, 1, MEM_HD))


def _mem_attn_kernel(q_ref, z_ref, k_ref, v_ref, qg_ref, o_ref):
    scale = 1.0 / math.sqrt(MEM_HD)
    for h in range(MEM_HEADS):
        sl = slice(h * MEM_HD, (h + 1) * MEM_HD)
        q = q_ref[0, :, sl].astype(F32)
        qn = (q * lax.rsqrt(jnp.mean(q * q, axis=-1, keepdims=True) + RMS_EPS) * (qg_ref[...] * scale)).astype(BF16)
        s = lax.dot_general(qn, k_ref[0, 0, :, sl], (((1,), (1,)), ((), ())), preferred_element_type=F32)
        p = jnp.exp(s - jnp.max(s, axis=-1, keepdims=True))
        l = jnp.sum(p, axis=-1, keepdims=True)
        o = jnp.dot(p.astype(BF16), v_ref[0, 0, :, sl], preferred_element_type=F32) / l
        o_ref[0, :, sl] = (o * _silu(z_ref[0, :, sl].astype(F32))).astype(BF16)


def _mem_attn(q, z, k_all, v_all, qg, layer, tq):
    b, s, _ = q.shape
    tok = pl.BlockSpec((1, tq, MEM_WIDTH), lambda i, j: (i, j, 0))
    bank = pl.BlockSpec((1, 1, N_MEM, MEM_WIDTH), lambda i, j: (layer, i, 0, 0))
    return pl.pallas_call(
        _mem_attn_kernel, grid=(b, s // tq),
        in_specs=[tok, tok, bank, bank, pl.BlockSpec((1, MEM_HD), lambda i, j: (0, 0))],
        out_specs=tok, out_shape=jax.ShapeDtypeStruct((b, s, MEM_WIDTH), BF16),
        compiler_params=_cparams(("parallel", "parallel")), name="mem_attn",
    )(q, z, k_all, v_all, qg.reshape(1, MEM_HD))


def _rope_table_kernel(pos_ref, inv_ref, cos_ref, sin_ref):
    ang = pos_ref[0].astype(F32) * inv_ref[...]
    cos_ref[0] = jnp.cos(ang)
    sin_ref[0] = jnp.sin(ang)


def _rope_tables(positions):
    b, s = positions.shape
    inv = ROPE_BASE ** (-jnp.arange(ROPE_HALF, dtype=F32) / ROPE_HALF)
    out = jax.ShapeDtypeStruct((b, ROPE_HALF, s), F32)
    cos, sin = pl.pallas_call(
        _rope_table_kernel, grid=(b,),
        in_specs=[pl.BlockSpec((1, 1, s), lambda i: (i, 0, 0)), pl.BlockSpec((ROPE_HALF, 1), lambda i: (0, 0))],
        out_specs=[pl.BlockSpec((1, ROPE_HALF, s), lambda i: (i, 0, 0))] * 2,
        out_shape=[out, out], compiler_params=_cparams(("parallel",)), name="rope_tables",
    )(positions.reshape(b, 1, s), inv.reshape(ROPE_HALF, 1))
    tail = LANES - MLA_QK
    cos_q = jnp.concatenate([cos, cos], 1)
    sin_q = jnp.concatenate([-sin, sin], 1)
    cos_k = jnp.concatenate([jnp.ones((b, s, MLA_NOPE), F32), jnp.swapaxes(cos_q, 1, 2), jnp.ones((b, s, tail), F32)], -1)
    sin_k = jnp.concatenate([jnp.zeros((b, s, MLA_NOPE), F32), jnp.swapaxes(sin_q, 1, 2), jnp.zeros((b, s, tail), F32)], -1)
    return cos_q, sin_q, cos_k.reshape(b * s, LANES), sin_k.reshape(b * s, LANES)


def _head_norm_rope(x, gain, cos, sin, lane):
    xn = x * lax.rsqrt(jnp.sum(x * x, axis=-1, keepdims=True) * (1.0 / MLA_QK) + RMS_EPS) * gain
    partner = jnp.where(lane < MLA_NOPE + ROPE_HALF,
                        pltpu.roll(xn, LANES - ROPE_HALF, 1), pltpu.roll(xn, ROPE_HALF, 1))
    return xn * cos + partner * sin


def _mla_prep_kernel(cq_ref, ckv_ref, kr_ref, cosq_ref, sinq_ref, cosk_ref, sink_ref, gqa_ref, wqt_ref, gkva_ref,
                     wk_ref, wvt_ref, gq_ref, gk_ref, qt_ref, k_ref, vt_ref):
    cq = cq_ref[...]
    cqn = cq * lax.rsqrt(jnp.mean(cq * cq, axis=-1, keepdims=True) + RMS_EPS) * gqa_ref[...]
    ckv = ckv_ref[...]
    ckvn = ckv * lax.rsqrt(jnp.mean(ckv * ckv, axis=-1, keepdims=True) + RMS_EPS) * gkva_ref[...]
    cqn_t = cqn.T.astype(BF16)
    ckvn_t = ckvn.T.astype(BF16)
    ckvn = ckvn.astype(BF16)
    kr = kr_ref[...]
    cosq, sinq, cosk, sink = cosq_ref[0], sinq_ref[0], cosk_ref[...], sink_ref[...]
    lane = lax.broadcasted_iota(jnp.int32, cosk.shape, 1)
    qscale = math.log2(math.e) / math.sqrt(MLA_QK)
    tm = cq.shape[0]
    ones_tile = jnp.where(lax.broadcasted_iota(jnp.int32, (MLA_VA - MLA_V, tm), 0) == 0, 1.0, 0.0).astype(BF16)
    for h in range(MLA_HEADS):
        sl = slice(h * LANES, (h + 1) * LANES)
        qt = jnp.dot(wqt_ref[sl, :], cqn_t, preferred_element_type=F32)
        qn = qt * lax.rsqrt(jnp.sum(qt * qt, axis=0, keepdims=True) * (1.0 / MLA_QK) + RMS_EPS) * gq_ref[...]
        lo, hi = h * LANES + MLA_NOPE, h * LANES + MLA_QK
        partner = jnp.concatenate([qn[MLA_NOPE + ROPE_HALF:MLA_QK], qn[MLA_NOPE:MLA_NOPE + ROPE_HALF]], axis=0)
        qt_ref[0, 0, h * LANES:lo, :] = (qn[:MLA_NOPE] * qscale).astype(BF16)
        qt_ref[0, 0, lo:hi, :] = ((qn[MLA_NOPE:MLA_QK] * cosq + partner * sinq) * qscale).astype(BF16)
        qt_ref[0, 0, hi:(h + 1) * LANES, :] = jnp.zeros((LANES - MLA_QK, tm), BF16)
        kh = jnp.dot(ckvn, wk_ref[:, sl], preferred_element_type=F32) + kr
        k_ref[:, sl] = _head_norm_rope(kh, gk_ref[...], cosk, sink, lane).astype(BF16)
        vs = slice(h * MLA_V, (h + 1) * MLA_V)
        vt_ref[0, 0, h * MLA_VA:h * MLA_VA + MLA_V, :] = jnp.dot(wvt_ref[vs, :], ckvn_t,
                                                                 preferred_element_type=F32).astype(BF16)
        vt_ref[0, 0, h * MLA_VA + MLA_V:(h + 1) * MLA_VA, :] = ones_tile


def _pad_heads(w, n_heads, width, offset=0):
    k = w.shape[0]
    w = w.reshape(k, n_heads, width)
    w = jnp.pad(w, ((0, 0), (0, 0), (offset, LANES - width - offset)))
    return w.reshape(k, n_heads * LANES)


def _mla_prep(cq, ckv, kr, rope_t, gqa, w_uq, gkva, w_ukv, gq, gk, b, s, tm):
    cos_q, sin_q, cos_k, sin_k = rope_t
    nt = s // tm
    wqt = _pad_heads(w_uq, MLA_HEADS, MLA_QK).T.astype(BF16)
    w_ukv = w_ukv.reshape(MLA_KV_LORA, MLA_HEADS, MLA_NOPE + MLA_V)
    wk = _pad_heads(w_ukv[:, :, :MLA_NOPE].reshape(MLA_KV_LORA, -1), MLA_HEADS, MLA_NOPE).astype(BF16)
    wvt = w_ukv[:, :, MLA_NOPE:].reshape(MLA_KV_LORA, MLA_WIDTH).T.astype(BF16)
    pad_gain = lambda g: jnp.pad(g, (0, LANES - MLA_QK))
    row = lambda i, j: (i * nt + j, 0)
    fixed = lambda i, j: (0, 0)
    hw = MLA_HEADS * LANES
    return pl.pallas_call(
        _mla_prep_kernel, grid=(b, nt),
        in_specs=[pl.BlockSpec((tm, MLA_Q_LORA), row), pl.BlockSpec((tm, MLA_KV_LORA), row),
                  pl.BlockSpec((tm, LANES), row),
                  pl.BlockSpec((1, MLA_ROPE, tm), lambda i, j: (i, 0, j)),
                  pl.BlockSpec((1, MLA_ROPE, tm), lambda i, j: (i, 0, j)),
                  pl.BlockSpec((tm, LANES), row), pl.BlockSpec((tm, LANES), row),
                  pl.BlockSpec((1, MLA_Q_LORA), fixed), pl.BlockSpec((hw, MLA_Q_LORA), fixed),
                  pl.BlockSpec((1, MLA_KV_LORA), fixed), pl.BlockSpec((MLA_KV_LORA, hw), fixed),
                  pl.BlockSpec((MLA_WIDTH, MLA_KV_LORA), fixed),
                  pl.BlockSpec((LANES, 1), fixed), pl.BlockSpec((1, LANES), fixed)],
        out_specs=[pl.BlockSpec((1, 1, hw, tm), lambda i, j: (i, j, 0, 0)), pl.BlockSpec((tm, hw), row),
                   pl.BlockSpec((1, 1, MLA_HEADS * MLA_VA, tm), lambda i, j: (i, j, 0, 0))],
        out_shape=[jax.ShapeDtypeStruct((b, nt, hw, tm), BF16), jax.ShapeDtypeStruct((b * s, hw), BF16),
                   jax.ShapeDtypeStruct((b, nt, MLA_HEADS * MLA_VA, tm), BF16)],
        compiler_params=_cparams(("parallel", "parallel")), name="mla_prep",
    )(cq, ckv, kr, cos_q, sin_q, cos_k, sin_k, gqa.reshape(1, -1), wqt, gkva.reshape(1, -1), wk, wvt,
      pad_gain(gq).reshape(LANES, 1), pad_gain(gk).reshape(1, LANES))


def _mla_attn_kernel(q_ref, k_ref, v_ref, z_ref, o_ref, m_sc, acc_sc, *, tq):
    qi = pl.program_id(2)
    m_sc[...] = jnp.full(m_sc.shape, -jnp.inf, F32)
    acc_sc[...] = jnp.zeros(acc_sc.shape, F32)

    def block(ki, masked):
        start = pl.multiple_of(ki * tq, tq)
        for h in range(2):
            s = jnp.dot(k_ref[0, pl.ds(start, tq), h * LANES:(h + 1) * LANES], q_ref[0, 0, h * LANES:(h + 1) * LANES, :],
                        preferred_element_type=F32)
            if masked:
                kc = lax.broadcasted_iota(jnp.int32, s.shape, 0) // CHUNK
                qc = lax.broadcasted_iota(jnp.int32, s.shape, 1) // CHUNK
                s = jnp.where(kc <= qc, s, -jnp.inf)
            m_prev = m_sc[h]
            m_new = jnp.maximum(m_prev, jnp.max(s, axis=0, keepdims=True))
            alpha = jnp.exp2(m_prev - m_new)
            p = jnp.exp2(s - m_new)
            acc_sc[h] = alpha * acc_sc[h] + jnp.dot(v_ref[0, ki, h * MLA_VA:(h + 1) * MLA_VA, :], p.astype(BF16),
                                                    preferred_element_type=F32)
            m_sc[h] = m_new

    def body(ki, carry):
        block(ki, masked=False)
        return carry

    lax.fori_loop(0, qi, body, 0)
    block(qi, masked=True)
    o_t = jnp.concatenate([acc_sc[h, :MLA_V, :] / acc_sc[h, MLA_V:MLA_V + 1, :] for h in range(2)], axis=0)
    o_ref[0] = (o_t.T * _silu(z_ref[0].astype(F32))).astype(BF16)


def _mla_attn(qt, k, vt, z, tq):
    b, s, _ = k.shape
    nt = s // tq
    pairs = MLA_HEADS // 2
    return pl.pallas_call(
        functools.partial(_mla_attn_kernel, tq=tq), grid=(b, pairs, nt),
        in_specs=[pl.BlockSpec((1, 1, 2 * LANES, tq), lambda i, p, j: (i, j, p, 0)),
                  pl.BlockSpec((1, s, 2 * LANES), lambda i, p, j: (i, 0, p)),
                  pl.BlockSpec((1, nt, 2 * MLA_VA, tq), lambda i, p, j: (i, 0, p, 0)),
                  pl.BlockSpec((1, tq, LANES), lambda i, p, j: (i, j, p))],
        out_specs=pl.BlockSpec((1, tq, LANES), lambda i, p, j: (i, j, p)),
        out_shape=jax.ShapeDtypeStruct((b, s, MLA_WIDTH), BF16),
        scratch_shapes=[pltpu.VMEM((2, 1, tq), F32), pltpu.VMEM((2, MLA_VA, tq), F32)],
        compiler_params=_cparams(("parallel", "parallel", "arbitrary")), name="mla_attn",
    )(qt, k, vt, z)


def _s5_tables(a_re, a_im, log_dt, b_re, b_im, c_re, c_im, n_steps):
    hp = lax.Precision.HIGHEST
    dt = jnp.exp(log_dt)[:, None]
    mag = jnp.exp(a_re * dt)
    ab_re, ab_im = mag * jnp.cos(a_im * dt), mag * jnp.sin(a_im * dt)
    den = a_re * a_re + a_im * a_im
    n_re, n_im = ab_re - 1.0, ab_im
    f_re = (n_re * a_re + n_im * a_im) / den
    f_im = (n_im * a_re - n_re * a_im) / den
    bb_re = f_re[..., None] * b_re - f_im[..., None] * b_im
    bb_im = f_re[..., None] * b_im + f_im[..., None] * b_re

    def lam_pow(k):
        k = k.astype(F32)[:, None, None]
        m = jnp.exp(k * a_re * dt)
        return m * jnp.cos(k * a_im * dt), m * jnp.sin(k * a_im * dt)

    L, C, P, G = S5_L, S5_GROUP, S5_STATE, S5_GROUPS
    lr, li = lam_pow(jnp.arange(L + 1))
    w_re = lr[..., None] * bb_re[None] - li[..., None] * bb_im[None]
    w_im = lr[..., None] * bb_im[None] + li[..., None] * bb_re[None]
    kk = (jnp.einsum("gdp,kgpc->kgdc", c_re, w_re[:L], precision=hp)
          - jnp.einsum("gdp,kgpc->kgdc", c_im, w_im[:L], precision=hp))
    lag = jnp.arange(L)[None, :] - jnp.arange(L)[:, None]
    kfull = jnp.where((lag >= 0)[:, :, None, None, None], kk[jnp.clip(lag, 0, L - 1)], 0.0)
    mt = jnp.transpose(kfull, (2, 0, 4, 1, 3)).reshape(G, L * C, L * C)
    e_re = jnp.transpose(w_re[:L][::-1], (1, 0, 3, 2)).reshape(G, L * C, P)
    e_im = jnp.transpose(w_im[:L][::-1], (1, 0, 3, 2)).reshape(G, L * C, P)
    w1 = jnp.concatenate([mt, e_re, e_im, -e_im, e_re], axis=-1)
    g_re = c_re[None] * lr[1:, :, None, :] - c_im[None] * li[1:, :, None, :]
    g_im = c_re[None] * li[1:, :, None, :] + c_im[None] * lr[1:, :, None, :]
    f_mat = jnp.concatenate([jnp.transpose(g_re, (1, 3, 0, 2)).reshape(G, P, L * C),
                             -jnp.transpose(g_im, (1, 3, 0, 2)).reshape(G, P, L * C)], axis=1)
    sr, si = lam_pow(L * (2 ** jnp.arange(n_steps)))
    la = jnp.transpose(jnp.concatenate([sr, sr], -1), (1, 0, 2))[:, :, None, :]
    lb = jnp.transpose(jnp.concatenate([si, si], -1), (1, 0, 2))[:, :, None, :]
    return w1.astype(BF16), f_mat.astype(BF16), la, lb


def _s5_kernel(u_ref, w1_ref, f_ref, la_ref, lb_ref, d_ref, y_ref, *, gb, n_steps):
    nc = u_ref.shape[2]
    row = lax.broadcasted_iota(jnp.int32, (nc, 2 * S5_STATE), 0)

    def shift(t, d):
        return jnp.where(row >= d, pltpu.roll(t, d, 0), 0.0)

    for g in range(gb):
        u = u_ref[0, g]
        r = jnp.dot(u, w1_ref[g], preferred_element_type=F32)
        y = r[:, :S5_FLAT]
        x = r[:, S5_FLAT:S5_FLAT + 2 * S5_STATE]
        xs = r[:, S5_FLAT + 2 * S5_STATE:]
        for s in range(n_steps):
            a, b = la_ref[g, s], lb_ref[g, s]
            tx, txs = a * x + b * xs, a * xs - b * x
            x, xs = x + shift(tx, 1 << s), xs + shift(txs, 1 << s)
        h_in = shift(x, 1).astype(BF16)
        y = y + jnp.dot(h_in, f_ref[g], preferred_element_type=F32) + d_ref[g] * u.astype(F32)
        y_ref[0, g] = jax.nn.gelu(y, approximate=True).astype(BF16)


def _s5(u, tables, d, gb):
    w1, f_mat, la, lb = tables
    b, s, _ = u.shape
    nc = s // S5_L
    n_steps = la.shape[1]
    G = S5_GROUPS
    uf = u.reshape(b, nc, S5_L, G, S5_GROUP).transpose(0, 3, 1, 2, 4).reshape(b, G, nc, S5_FLAT)
    d_flat = jnp.tile(d.reshape(G, 1, S5_GROUP), (1, 1, S5_L))
    grp = lambda i, j: (j, 0, 0)
    y = pl.pallas_call(
        functools.partial(_s5_kernel, gb=gb, n_steps=n_steps), grid=(b, G // gb),
        in_specs=[pl.BlockSpec((1, gb, nc, S5_FLAT), lambda i, j: (i, j, 0, 0)),
                  pl.BlockSpec((gb, S5_FLAT, 2 * S5_FLAT), grp),
                  pl.BlockSpec((gb, 2 * S5_STATE, S5_FLAT), grp),
                  pl.BlockSpec((gb, n_steps, 1, 2 * S5_STATE), lambda i, j: (j, 0, 0, 0)),
                  pl.BlockSpec((gb, n_steps, 1, 2 * S5_STATE), lambda i, j: (j, 0, 0, 0)),
                  pl.BlockSpec((gb, 1, S5_FLAT), grp)],
        out_specs=pl.BlockSpec((1, gb, nc, S5_FLAT), lambda i, j: (i, j, 0, 0)),
        out_shape=jax.ShapeDtypeStruct((b, G, nc, S5_FLAT), BF16),
        compiler_params=_cparams(("parallel", "parallel")), name="s5",
    )(uf, w1, f_mat, la, lb, d_flat)
    return y.reshape(b, G, nc, S5_L, S5_GROUP).transpose(0, 2, 3, 1, 4).reshape(b, s, S5_WIDTH)


def _split_dot(x, e):
    hi = x.astype(BF16)
    lo = (x - hi.astype(F32)).astype(BF16)
    return jnp.dot(hi, e, preferred_element_type=F32) + jnp.dot(lo, e, preferred_element_type=F32)


def _ssd_kernel(xc_ref, dt_ref, dtb_ref, alog_ref, dexp_ref, e_ref, y_ref, state):
    L, PAD = M2_CHUNK, 8

    @pl.when(pl.program_id(1) == 0)
    def _():
        state[...] = jnp.zeros_like(state)

    xc = xc_ref[0].astype(F32)
    xs = xc[:, :M2_INNER]
    nbc = M2_GROUPS * M2_STATE
    bm = xc[:, M2_INNER:M2_INNER + nbc]
    cm = xc[:, M2_INNER + nbc:]

    dt = jax.nn.softplus(dt_ref[0] + dtb_ref[...])
    da = dt * (-jnp.exp(alog_ref[...]))
    ti = lax.broadcasted_iota(jnp.int32, (L, L), 0)
    si = lax.broadcasted_iota(jnp.int32, (L, L), 1)
    causal = si <= ti
    tril =jnp.where(causal, 1.0, 0.0).astype(BF16)
    hi = da.astype(BF16)
    r1 = da - hi.astype(F32)
    mid = r1.astype(BF16)
    lo = (r1 - mid.astype(F32)).astype(BF16)
    cs = (jnp.dot(tril, hi, preferred_element_type=F32) + jnp.dot(tril, mid, preferred_element_type=F32)
          + jnp.dot(tril, lo, preferred_element_type=F32))
    cs2 = cs * math.log2(math.e)
    cs2_t = cs2.T
    cs_end = cs[L - 1:L, :]
    stack = jnp.concatenate([dt, dt * jnp.exp(cs_end - cs), jnp.exp(cs)], axis=0).astype(BF16)
    fac = jnp.dot(stack, e_ref[...], preferred_element_type=F32)
    dt_e, dw_e, ecs_e = fac[:L], fac[L:2 * L], fac[2 * L:]
    dend_e = _split_dot(jnp.broadcast_to(jnp.exp(cs_end), (PAD, LANES)), e_ref[...])[0:1]
    x_dt = (xs * dt_e).astype(BF16)
    x_w = (xs * dw_e).astype(BF16)

    lane = lax.broadcasted_iota(jnp.int32, (L, LANES), 1)
    zero = jnp.zeros((L, LANES), BF16)
    for g in range(M2_GROUPS):
        gs = slice(g * M2_GW, (g + 1) * M2_GW)
        b_g = bm[:, g * M2_STATE:(g + 1) * M2_STATE]
        c_g = cm[:, g * M2_STATE:(g + 1) * M2_STATE].astype(BF16)
        cb = lax.dot_general(c_g, b_g.astype(BF16), (((1,), (1,)), ((), ())), preferred_element_type=F32)
        s_old = state[g]
        y_off = jnp.dot(c_g, s_old.astype(BF16), preferred_element_type=F32) * ecs_e[:, gs]
        state[g] = s_old * dend_e[:, gs] + jnp.dot(b_g.T.astype(BF16), x_w[:, gs], preferred_element_type=F32)
        for pr in range(M2_GW // LANES):
            ps = slice(g * M2_GW + pr * LANES, g * M2_GW + (pr + 1) * LANES)
            xp = x_dt[:, ps]
            y_pair = None
            for hh in range(2):
                h = (g * M2_GW + pr * LANES) // M2_HEADDIM + hh
                seg = jnp.exp2(jnp.where(causal, cs2[:, h:h + 1] - cs2_t[h:h + 1, :], -jnp.inf))
                att = (cb * seg).astype(BF16)
                x_h = jnp.where((lane < M2_HEADDIM) == (hh == 0), xp, zero)
                part = jnp.dot(att, x_h, preferred_element_type=F32)
                y_pair = part if y_pair is None else y_pair + part
            y = y_pair + y_off[:, pr * LANES:(pr + 1) * LANES] + dexp_ref[:, ps] * xs[:, ps]
            y_ref[0, :, ps] = y.astype(y_ref.dtype)


def _ssd(xc, dt_raw, dt_bias, a_log, d):
    b, s, _ = xc.shape
    pad_row = lambda v: jnp.pad(v, (0, LANES - M2_HEADS)).reshape(1, LANES)
    expand = (jnp.arange(LANES)[:, None] == (jnp.arange(M2_INNER) // M2_HEADDIM)[None, :]).astype(BF16)
    tok = lambda w: pl.BlockSpec((1, M2_CHUNK, w), lambda i, j: (i, j, 0))
    fixed = lambda shape: pl.BlockSpec(shape, lambda i, j: (0, 0))
    return pl.pallas_call(
        _ssd_kernel, grid=(b, s // M2_CHUNK),
        in_specs=[tok(M2_CONV_DIM), tok(LANES), fixed((1, LANES)), fixed((1, LANES)), fixed((1, M2_INNER)),
                  fixed((LANES, M2_INNER))],
        out_specs=tok(M2_INNER), out_shape=jax.ShapeDtypeStruct((b, s, M2_INNER), BF16),
        scratch_shapes=[pltpu.VMEM((M2_GROUPS, M2_STATE, M2_GW), F32)],
        compiler_params=_cparams(("parallel", "arbitrary")), name="ssd",
    )(xc, dt_raw, pad_row(dt_bias), pad_row(a_log), jnp.repeat(d, M2_HEADDIM).reshape(1, M2_INNER), expand)


def _cols(w, sizes):
    idx, out = 0, []
    for n in sizes:
        out.append(w[:, idx:idx + n])
        idx += n
    return out


def _even_layer(h, b, s, i, layer, p, mem_k, mem_v, rope_t, tm, tq):
    w_u, w_za, w_cq, w_ckv, w_kr, w_zb, w_qm, w_zm = _cols(
        p["ev_w_in"][i], (S5_WIDTH, S5_WIDTH, MLA_Q_LORA, MLA_KV_LORA, MLA_ROPE, MLA_WIDTH, MEM_WIDTH, MEM_WIDTH))
    w_kr = jnp.pad(w_kr, ((0, 0), (MLA_NOPE, LANES - MLA_QK)))
    weights = [w.astype(BF16) for w in (w_u, w_za, w_zb, w_qm, w_zm, w_cq, w_ckv, w_kr)]
    u, za, zb, qm, zm, cq, ckv, kr = _rms_proj(h, p["norm_g"][layer], weights, [BF16] * 5 + [F32] * 3, tm)

    n_steps = max(1, (s // S5_L - 1).bit_length())
    tables = _s5_tables(p["s5_a_re"][i], p["s5_a_im"][i], p["s5_log_dt"][i], p["s5_b_re"][i], p["s5_b_im"][i],
                        p["s5_c_re"][i], p["s5_c_im"][i], n_steps)
    y_s5 = _s5(u.reshape(b, s, S5_WIDTH), tables, p["s5_d"][i], gb=8).reshape(b * s, S5_WIDTH)

    qt, k, vt = _mla_prep(cq, ckv, kr, rope_t, p["mla_q_a_norm_g"][i], p["mla_w_uq"][i], p["mla_kv_a_norm_g"][i],
                          p["mla_w_ukv"][i], p["mla_q_norm_g"][i], p["mla_k_norm_g"][i], b, s, tq)
    y_b = _mla_attn(qt, k.reshape(b, s, MLA_HEADS * LANES), vt, zb.reshape(b, s, MLA_WIDTH),
                    tq).reshape(b * s, MLA_WIDTH)

    y_m = _mem_attn(qm.reshape(b, s, MEM_WIDTH), zm.reshape(b, s, MEM_WIDTH), mem_k, mem_v,
                    p["mem_q_norm_g"][layer], layer, 2 * tm if s % (2 * tm) == 0 else tm).reshape(b * s, MEM_WIDTH)

    w_out = p["ev_w_out"][i].astype(BF16)
    ws = [w_out[:S5_WIDTH], w_out[S5_WIDTH:S5_WIDTH + MLA_WIDTH], w_out[S5_WIDTH + MLA_WIDTH:]]
    glu = (za, p["s5_glu_w"][i].astype(BF16), p["s5_glu_b"][i].reshape(1, S5_WIDTH))
    return _out_proj(h, [y_s5, y_b, y_m], ws, tm, glu)


def _odd_layer(h, b, s, i, layer, p, mem_k, mem_v, tm):
    w_z, w_xbc, w_dt, w_qm, w_zm = _cols(p["od_w_in"][i], (M2_INNER, M2_CONV_DIM, M2_HEADS, MEM_WIDTH, MEM_WIDTH))
    w_dt = jnp.pad(w_dt, ((0, 0), (0, LANES - M2_HEADS)))
    weights = [w.astype(BF16) for w in (w_z, w_xbc, w_qm, w_zm, w_dt)]
    z, xc, qm, zm, dt_raw = _rms_proj(h, p["norm_g"][layer], weights, [BF16] * 4 + [F32], tm,
                                      conv=(1, p["m2_conv_w"][i], p["m2_conv_b"][i]), seq_len=s)
    y_c = _ssd(xc.reshape(b, s, M2_CONV_DIM), dt_raw.reshape(b, s, LANES), p["m2_dt_bias"][i], p["m2_a_log"][i],
               p["m2_d"][i]).reshape(b * s, M2_INNER)
    y_m = _mem_attn(qm.reshape(b, s, MEM_WIDTH), zm.reshape(b, s, MEM_WIDTH), mem_k, mem_v,
                    p["mem_q_norm_g"][layer], layer, 2 * tm if s % (2 * tm) == 0 else tm).reshape(b * s, MEM_WIDTH)
    w_out = p["od_w_out"][i].astype(BF16)
    return _out_proj(h, [y_c, y_m], [w_out[:M2_INNER], w_out[M2_INNER:]], tm,
                     m2_args=(z, p["m2_norm_g"][i].reshape(1, M2_INNER)))


def _token_tile(s):
    return 512 if s % 512 == 0 else s


def kernel(x, mem, positions, norm_g, mem_norm_g, mem_w_kv, mem_q_norm_g, mem_k_norm_g, ev_w_in, ev_w_out, s5_a_re, s5_a_im, s5_log_dt, s5_b_re, s5_b_im, s5_c_re, s5_c_im, s5_d, s5_glu_w, s5_glu_b, mla_q_a_norm_g, mla_w_uq, mla_kv_a_norm_g, mla_w_ukv, mla_q_norm_g, mla_k_norm_g, od_w_in, od_w_out, m2_conv_w, m2_conv_b, m2_dt_bias, m2_a_log, m2_d, m2_norm_g):
    p = dict(norm_g=norm_g, mem_q_norm_g=mem_q_norm_g, ev_w_in=ev_w_in, ev_w_out=ev_w_out,
             s5_a_re=s5_a_re, s5_a_im=s5_a_im, s5_log_dt=s5_log_dt, s5_b_re=s5_b_re, s5_b_im=s5_b_im,
             s5_c_re=s5_c_re, s5_c_im=s5_c_im, s5_d=s5_d, s5_glu_w=s5_glu_w, s5_glu_b=s5_glu_b,
             mla_q_a_norm_g=mla_q_a_norm_g, mla_w_uq=mla_w_uq, mla_kv_a_norm_g=mla_kv_a_norm_g,
             mla_w_ukv=mla_w_ukv, mla_q_norm_g=mla_q_norm_g, mla_k_norm_g=mla_k_norm_g,
             od_w_in=od_w_in, od_w_out=od_w_out, m2_conv_w=m2_conv_w, m2_conv_b=m2_conv_b,
             m2_dt_bias=m2_dt_bias, m2_a_log=m2_a_log, m2_d=m2_d, m2_norm_g=m2_norm_g)
    b, s, d = x.shape
    tm = _token_tile(s)
    tq = 1024 if s % 1024 == 0 else s
    mem_k, mem_v = _mem_kv(mem, mem_norm_g, mem_w_kv, mem_k_norm_g)
    rope_t = _rope_tables(positions)
    h = x.reshape(b * s, d)
    for layer in range(DEPTH):
        if layer % 2 == 0:
            h = _even_layer(h, b, s, layer // 2, layer, p, mem_k, mem_v, rope_t, tm, tq)
        else:
            h = _odd_layer(h, b, s, layer // 2, layer, p, mem_k, mem_v, tm)
    return h.reshape(b, s, d)
```

```python
import functools
import math

import jax
import jax.numpy as jnp
from jax import lax
from jax.experimental import pallas as pl
from jax.experimental.pallas import tpu as pltpu

F32 = jnp.float32
BF16 = jnp.bfloat16

D_MODEL = 1024
DEPTH = 4
CHUNK = 64
N_MEM = 256
RMS_EPS = 1e-6

S5_WIDTH = 512
S5_GROUP = 16
S5_GROUPS = S5_WIDTH // S5_GROUP
S5_STATE = 64
S5_L = 16
S5_FLAT = S5_L * S5_GROUP

MLA_HEADS = 8
MLA_NOPE = 64
MLA_ROPE = 32
MLA_QK = MLA_NOPE + MLA_ROPE
MLA_V = 64
MLA_WIDTH = MLA_HEADS * MLA_V
MLA_HPS = 2
MLA_VA = MLA_V + 16
MLA_Q_LORA = 256
MLA_KV_LORA = 128
ROPE_BASE = 10000.0
ROPE_HALF = MLA_ROPE // 2
LANES = 128

M2_INNER = 2 * D_MODEL
M2_HEADDIM = 64
M2_HEADS = M2_INNER // M2_HEADDIM
M2_GROUPS = 4
M2_STATE = 128
M2_CONV = 4
M2_CHUNK = 128
M2_CONV_DIM = M2_INNER + 2 * M2_GROUPS * M2_STATE
M2_GW = M2_INNER // M2_GROUPS

MEM_HEADS = 4
MEM_HD = 128
MEM_WIDTH = MEM_HEADS * MEM_HD

VMEM_LIMIT = 56 * 1024 * 1024


def _cparams(sem):
    return pltpu.CompilerParams(dimension_semantics=sem, vmem_limit_bytes=VMEM_LIMIT)


def _silu(z):
    h = 0.5 * z
    return h + h * jnp.tanh(h)


CONV_PAD = 8


def _rms_proj_kernel(h_ref, g_ref, *refs, n_out, col_chunk, conv_idx, tiles_per_seq):
    w_refs = refs[:n_out]
    if conv_idx is None:
        o_refs = refs[n_out:]
    else:
        cw_ref, cb_ref = refs[n_out:n_out + 2]
        o_refs = refs[n_out + 2:2 * n_out + 2]
        pad, tail = refs[2 * n_out + 2:]

        @pl.when(pl.program_id(0) % tiles_per_seq == 0)
        def _():
            tail[...] = jnp.zeros_like(tail)

    x = h_ref[...]
    tm = x.shape[0]
    xn = (x * lax.rsqrt(jnp.mean(x * x, axis=-1, keepdims=True) + RMS_EPS) * g_ref[...]).astype(BF16)
    for i, (w_ref, o_ref) in enumerate(zip(w_refs, o_refs)):
        n = w_ref.shape[1]
        for c0 in range(0, n, col_chunk):
            c1 = min(n, c0 + col_chunk)
            r = jnp.dot(xn, w_ref[:, c0:c1], preferred_element_type=F32)
            if i == conv_idx:
                pad[0:CONV_PAD, :] = tail[:, c0:c1]
                pad[CONV_PAD:CONV_PAD + tm, :] = r
                first = CONV_PAD - (M2_CONV - 1)
                acc = cb_ref[:, c0:c1] + cw_ref[0:1, c0:c1] * pad[first:first + tm, :]
                for k in range(1, M2_CONV):
                    acc = acc + cw_ref[k:k + 1, c0:c1] * pad[first + k:first + k + tm, :]
                tail[:, c0:c1] = pad[tm:tm + CONV_PAD, :]
                r = _silu(acc)
            o_ref[:, c0:c1] = r.astype(o_ref.dtype)


def _rms_proj(h, g, weights, out_dtypes, tm, conv=None, seq_len=None):
    t, d = h.shape
    col_chunk = 256 if conv is not None else 512
    fixed = lambda i: (0, 0)
    in_specs = [pl.BlockSpec((tm, d), lambda i: (i, 0)), pl.BlockSpec((1, d), fixed)]
    in_specs += [pl.BlockSpec(w.shape, fixed) for w in weights]
    args = [h, g.reshape(1, d), *weights]
    scratch, conv_idx, tiles_per_seq = [], None, None
    if conv is not None:
        conv_idx, cw, cb = conv
        n = weights[conv_idx].shape[1]
        in_specs += [pl.BlockSpec(cw.shape, fixed), pl.BlockSpec((1, n), fixed)]
        args += [cw, cb.reshape(1, n)]
        scratch = [pltpu.VMEM((tm + CONV_PAD, col_chunk), F32), pltpu.VMEM((CONV_PAD, n), F32)]
        tiles_per_seq = seq_len // tm
    out_specs = [pl.BlockSpec((tm, w.shape[1]), lambda i: (i, 0)) for w in weights]
    out_shape = [jax.ShapeDtypeStruct((t, w.shape[1]), dt) for w, dt in zip(weights, out_dtypes)]
    return pl.pallas_call(
        functools.partial(_rms_proj_kernel, n_out=len(weights), col_chunk=col_chunk, conv_idx=conv_idx,
                          tiles_per_seq=tiles_per_seq),
        grid=(t // tm,), in_specs=in_specs, out_specs=out_specs, out_shape=out_shape, scratch_shapes=scratch,
        compiler_params=_cparams(("arbitrary",)), name="rms_proj",
    )(*args)


def _out_proj_kernel(h_ref, *refs, n_in, pre):
    x_refs, w_refs = refs[:n_in], refs[n_in:2 * n_in]
    extra = refs[2 * n_in:-1]
    o_ref = refs[-1]
    acc = h_ref[...]
    for i, (x_ref, w_ref) in enumerate(zip(x_refs, w_refs)):
        if i == 0 and pre == "m2":
            z_ref, ng_ref = extra
            for g in range(M2_GROUPS):
                gs = slice(g * M2_GW, (g + 1) * M2_GW)
                gated = x_ref[:, gs].astype(F32) * _silu(z_ref[:, gs].astype(F32))
                xn = gated * lax.rsqrt(jnp.mean(gated * gated, axis=-1, keepdims=True) + RMS_EPS) * ng_ref[:, gs]
                acc = acc + jnp.dot(xn.astype(BF16), w_ref[gs, :], preferred_element_type=F32)
            continue
        x = x_ref[...]
        if i == 0 and pre == "glu":
            za_ref, gw_ref, gb_ref = extra
            gate = jnp.dot(x, gw_ref[...], preferred_element_type=F32) + gb_ref[...]
            x = (x.astype(F32) * jax.nn.sigmoid(gate) * _silu(za_ref[...].astype(F32))).astype(BF16)
        acc = acc + jnp.dot(x, w_ref[...], preferred_element_type=F32)
    o_ref[...] = acc


def _out_proj(h, xs, ws, tm, glu_args=None, m2_args=None):
    t, d = h.shape
    row = lambda i: (i, 0)
    fixed = lambda i: (0, 0)
    in_specs = [pl.BlockSpec((tm, d), row)]
    in_specs += [pl.BlockSpec((tm, x.shape[1]), row) for x in xs]
    in_specs += [pl.BlockSpec(w.shape, fixed) for w in ws]
    args = [h, *xs, *ws]
    if glu_args is not None:
        za, gw, gb = glu_args
        in_specs += [pl.BlockSpec((tm, za.shape[1]), row), pl.BlockSpec(gw.shape, fixed),
                     pl.BlockSpec(gb.shape, fixed)]
        args += [za, gw, gb]
    if m2_args is not None:
        z, ng = m2_args
        in_specs += [pl.BlockSpec((tm, z.shape[1]), row), pl.BlockSpec(ng.shape, fixed)]
        args += [z, ng]
    pre = "glu" if glu_args is not None else "m2" if m2_args is not None else None
    return pl.pallas_call(
        functools.partial(_out_proj_kernel, n_in=len(xs), pre=pre),
        grid=(t // tm,), in_specs=in_specs, out_specs=pl.BlockSpec((tm, d), row),
        out_shape=jax.ShapeDtypeStruct((t, d), F32),
        compiler_params=_cparams(("parallel",)), name="out_proj",
    )(*args)


def _mem_kv_kernel(mem_ref, g_ref, w_ref, kg_ref, k_ref, v_ref):
    x = mem_ref[0]
    xn = (x * lax.rsqrt(jnp.mean(x * x, axis=-1, keepdims=True) + RMS_EPS) * g_ref[0]).astype(BF16)
    kv = jnp.dot(xn, w_ref[0], preferred_element_type=F32)
    for h in range(MEM_HEADS):
        kh = kv[:, h * MEM_HD:(h + 1) * MEM_HD]
        kn = kh * lax.rsqrt(jnp.mean(kh * kh, axis=-1, keepdims=True) + RMS_EPS) * kg_ref[0]
        k_ref[0, 0, :, h * MEM_HD:(h + 1) * MEM_HD] = kn.astype(BF16)
    v_ref[0, 0] = kv[:, MEM_WIDTH:].astype(BF16)


def _mem_kv(mem, mem_norm_g, w_kv, k_norm_g):
    b = mem.shape[0]
    out = jax.ShapeDtypeStruct((DEPTH, b, N_MEM, MEM_WIDTH), BF16)
    return pl.pallas_call(
        _mem_kv_kernel, grid=(DEPTH, b),
        in_specs=[pl.BlockSpec((1, N_MEM, D_MODEL), lambda l, i: (i, 0, 0)),
                  pl.BlockSpec((1, 1, D_MODEL), lambda l, i: (l, 0, 0)),
                  pl.BlockSpec((1, D_MODEL, 2 * MEM_WIDTH), lambda l, i: (l, 0, 0)),
                  pl.BlockSpec((1, 1, MEM_HD), lambda l, i: (l, 0, 0))],
        out_specs=[pl.BlockSpec((1, 1, N_MEM, MEM_WIDTH), lambda l, i: (l, i, 0, 0))] * 2,
        out_shape=[out, out], compiler_params=_cparams(("arbitrary", "arbitrary")), name="mem_kv",
    )(mem, mem_norm_g.reshape(DEPTH, 1, D_MODEL), w_kv.astype(BF16), k_norm_g.reshape(DEPTH, 1, MEM_HD))


def _mem_attn_kernel(q_ref, z_ref, k_ref, v_ref, qg_ref, o_ref):
    scale = 1.0 / math.sqrt(MEM_HD)
    for h in range(MEM_HEADS):
        sl = slice(h * MEM_HD, (h + 1) * MEM_HD)
        q = q_ref[0, :, sl].astype(F32)
        qn = (q * lax.rsqrt(jnp.mean(q * q, axis=-1, keepdims=True) + RMS_EPS) * (qg_ref[...] * scale)).astype(BF16)
        s = lax.dot_general(qn, k_ref[0, 0, :, sl], (((1,), (1,)), ((), ())), preferred_element_type=F32)
        p = jnp.exp(s - jnp.max(s, axis=-1, keepdims=True))
        l = jnp.sum(p, axis=-1, keepdims=True)
        o = jnp.dot(p.astype(BF16), v_ref[0, 0, :, sl], preferred_element_type=F32) / l
        o_ref[0, :, sl] = (o * _silu(z_ref[0, :, sl].astype(F32))).astype(BF16)


def _mem_attn(q, z, k_all, v_all, qg, layer, tq):
    b, s, _ = q.shape
    tok = pl.BlockSpec((1, tq, MEM_WIDTH), lambda i, j: (i, j, 0))
    bank = pl.BlockSpec((1, 1, N_MEM, MEM_WIDTH), lambda i, j: (layer, i, 0, 0))
    return pl.pallas_call(
        _mem_attn_kernel, grid=(b, s // tq),
        in_specs=[tok, tok, bank, bank, pl.BlockSpec((1, MEM_HD), lambda i, j: (0, 0))],
        out_specs=tok, out_shape=jax.ShapeDtypeStruct((b, s, MEM_WIDTH), BF16),
        compiler_params=_cparams(("parallel", "parallel")), name="mem_attn",
    )(q, z, k_all, v_all, qg.reshape(1, MEM_HD))


def _rope_table_kernel(pos_ref, inv_ref, cos_ref, sin_ref):
    ang = pos_ref[0].astype(F32) * inv_ref[...]
    cos_ref[0] = jnp.cos(ang)
    sin_ref[0] = jnp.sin(ang)


def _rope_tables(positions):
    b, s = positions.shape
    inv = ROPE_BASE ** (-jnp.arange(ROPE_HALF, dtype=F32) / ROPE_HALF)
    out = jax.ShapeDtypeStruct((b, ROPE_HALF, s), F32)
    cos, sin = pl.pallas_call(
        _rope_table_kernel, grid=(b,),
        in_specs=[pl.BlockSpec((1, 1, s), lambda i: (i, 0, 0)), pl.BlockSpec((ROPE_HALF, 1), lambda i: (0, 0))],
        out_specs=[pl.BlockSpec((1, ROPE_HALF, s), lambda i: (i, 0, 0))] * 2,
        out_shape=[out, out], compiler_params=_cparams(("parallel",)), name="rope_tables",
    )(positions.reshape(b, 1, s), inv.reshape(ROPE_HALF, 1))
    tail = LANES - MLA_QK
    cos_q = jnp.concatenate([cos, cos], 1)
    sin_q = jnp.concatenate([-sin, sin], 1)
    cos_k = jnp.concatenate([jnp.ones((b, s, MLA_NOPE), F32), jnp.swapaxes(cos_q, 1, 2), jnp.ones((b, s, tail), F32)], -1)
    sin_k = jnp.concatenate([jnp.zeros((b, s, MLA_NOPE), F32), jnp.swapaxes(sin_q, 1, 2), jnp.zeros((b, s, tail), F32)], -1)
    return cos_q, sin_q, cos_k.reshape(b * s, LANES), sin_k.reshape(b * s, LANES)


def _head_norm_rope(x, gain, cos, sin, lane):
    xn = x * lax.rsqrt(jnp.sum(x * x, axis=-1, keepdims=True) * (1.0 / MLA_QK) + RMS_EPS) * gain
    partner = jnp.where(lane < MLA_NOPE + ROPE_HALF,
                        pltpu.roll(xn, LANES - ROPE_HALF, 1), pltpu.roll(xn, ROPE_HALF, 1))
    return xn * cos + partner * sin


def _mla_prep_kernel(cq_ref, ckv_ref, kr_ref, cosq_ref, sinq_ref, cosk_ref, sink_ref, gqa_ref, wqt_ref, gkva_ref,
                     wk_ref, wvt_ref, gq_ref, gk_ref, qt_ref, k_ref, vt_ref):
    cq = cq_ref[...]
    cqn = cq * lax.rsqrt(jnp.mean(cq * cq, axis=-1, keepdims=True) + RMS_EPS) * gqa_ref[...]
    ckv = ckv_ref[...]
    ckvn = ckv * lax.rsqrt(jnp.mean(ckv * ckv, axis=-1, keepdims=True) + RMS_EPS) * gkva_ref[...]
    cqn_t = cqn.T.astype(BF16)
    ckvn_t = ckvn.T.astype(BF16)
    ckvn = ckvn.astype(BF16)
    kr = kr_ref[...]
    cosq, sinq, cosk, sink = cosq_ref[0], sinq_ref[0], cosk_ref[...], sink_ref[...]
    lane = lax.broadcasted_iota(jnp.int32, cosk.shape, 1)
    qscale = math.log2(math.e) / math.sqrt(MLA_QK)
    tm = cq.shape[0]
    ones_tile = jnp.where(lax.broadcasted_iota(jnp.int32, (MLA_VA - MLA_V, tm), 0) == 0, 1.0, 0.0).astype(BF16)
    for h in range(MLA_HEADS):
        sl = slice(h * LANES, (h + 1) * LANES)
        qt = jnp.dot(wqt_ref[sl, :], cqn_t, preferred_element_type=F32)
        qn = qt * lax.rsqrt(jnp.sum(qt * qt, axis=0, keepdims=True) * (1.0 / MLA_QK) + RMS_EPS) * gq_ref[...]
        lo, hi = h * LANES + MLA_NOPE, h * LANES + MLA_QK
        partner = jnp.concatenate([qn[MLA_NOPE + ROPE_HALF:MLA_QK], qn[MLA_NOPE:MLA_NOPE + ROPE_HALF]], axis=0)
        qt_ref[0, 0, h * LANES:lo, :] = (qn[:MLA_NOPE] * qscale).astype(BF16)
        qt_ref[0, 0, lo:hi, :] = ((qn[MLA_NOPE:MLA_QK] * cosq + partner * sinq) * qscale).astype(BF16)
        qt_ref[0, 0, hi:(h + 1) * LANES, :] = jnp.zeros((LANES - MLA_QK, tm), BF16)
        kh = jnp.dot(ckvn, wk_ref[:, sl], preferred_element_type=F32) + kr
        k_ref[:, sl] = _head_norm_rope(kh, gk_ref[...], cosk, sink, lane).astype(BF16)
        vs = slice(h * MLA_V, (h + 1) * MLA_V)
        vt_ref[0, 0, h * MLA_VA:h * MLA_VA + MLA_V, :] = jnp.dot(wvt_ref[vs, :], ckvn_t,
                                                                 preferred_element_type=F32).astype(BF16)
        vt_ref[0, 0, h * MLA_VA + MLA_V:(h + 1) * MLA_VA, :] = ones_tile


def _pad_heads(w, n_heads, width, offset=0):
    k = w.shape[0]
    w = w.reshape(k, n_heads, width)
    w = jnp.pad(w, ((0, 0), (0, 0), (offset, LANES - width - offset)))
    return w.reshape(k, n_heads * LANES)


def _mla_prep(cq, ckv, kr, rope_t, gqa, w_uq, gkva, w_ukv, gq, gk, b, s, tm):
    cos_q, sin_q, cos_k, sin_k = rope_t
    nt = s // tm
    wqt = _pad_heads(w_uq, MLA_HEADS, MLA_QK).T.astype(BF16)
    w_ukv = w_ukv.reshape(MLA_KV_LORA, MLA_HEADS, MLA_NOPE + MLA_V)
    wk = _pad_heads(w_ukv[:, :, :MLA_NOPE].reshape(MLA_KV_LORA, -1), MLA_HEADS, MLA_NOPE).astype(BF16)
    wvt = w_ukv[:, :, MLA_NOPE:].reshape(MLA_KV_LORA, MLA_WIDTH).T.astype(BF16)
    pad_gain = lambda g: jnp.pad(g, (0, LANES - MLA_QK))
    row = lambda i, j: (i * nt + j, 0)
    fixed = lambda i, j: (0, 0)
    hw = MLA_HEADS * LANES
    return pl.pallas_call(
        _mla_prep_kernel, grid=(b, nt),
        in_specs=[pl.BlockSpec((tm, MLA_Q_LORA), row), pl.BlockSpec((tm, MLA_KV_LORA), row),
                  pl.BlockSpec((tm, LANES), row),
                  pl.BlockSpec((1, MLA_ROPE, tm), lambda i, j: (i, 0, j)),
                  pl.BlockSpec((1, MLA_ROPE, tm), lambda i, j: (i, 0, j)),
                  pl.BlockSpec((tm, LANES), row), pl.BlockSpec((tm, LANES), row),
                  pl.BlockSpec((1, MLA_Q_LORA), fixed), pl.BlockSpec((hw, MLA_Q_LORA), fixed),
                  pl.BlockSpec((1, MLA_KV_LORA), fixed), pl.BlockSpec((MLA_KV_LORA, hw), fixed),
                  pl.BlockSpec((MLA_WIDTH, MLA_KV_LORA), fixed),
                  pl.BlockSpec((LANES, 1), fixed), pl.BlockSpec((1, LANES), fixed)],
        out_specs=[pl.BlockSpec((1, 1, hw, tm), lambda i, j: (i, j, 0, 0)), pl.BlockSpec((tm, hw), row),
                   pl.BlockSpec((1, 1, MLA_HEADS * MLA_VA, tm), lambda i, j: (i, j, 0, 0))],
        out_shape=[jax.ShapeDtypeStruct((b, nt, hw, tm), BF16), jax.ShapeDtypeStruct((b * s, hw), BF16),
                   jax.ShapeDtypeStruct((b, nt, MLA_HEADS * MLA_VA, tm), BF16)],
        compiler_params=_cparams(("parallel", "parallel")), name="mla_prep",
    )(cq, ckv, kr, cos_q, sin_q, cos_k, sin_k, gqa.reshape(1, -1), wqt, gkva.reshape(1, -1), wk, wvt,
      pad_gain(gq).reshape(LANES, 1), pad_gain(gk).reshape(1, LANES))


def _mla_attn_kernel(q_ref, k_ref, v_ref, z_ref, o_ref, m_sc, acc_sc, *, tq):
    qi = pl.program_id(2)
    m_sc[...] = jnp.full(m_sc.shape, -jnp.inf, F32)
    acc_sc[...] = jnp.zeros(acc_sc.shape, F32)

    def block(ki, masked):
        start = pl.multiple_of(ki * tq, tq)
        for h in range(MLA_HPS):
            s = jnp.dot(k_ref[0, pl.ds(start, tq), h * LANES:(h + 1) * LANES], q_ref[0, 0, h * LANES:(h + 1) * LANES, :],
                        preferred_element_type=F32)
            if masked:
                kc = lax.broadcasted_iota(jnp.int32, s.shape, 0) // CHUNK
                qc = lax.broadcasted_iota(jnp.int32, s.shape, 1) // CHUNK
                s = jnp.where(kc <= qc, s, -jnp.inf)
            m_prev = m_sc[h]
            m_new = jnp.maximum(m_prev, jnp.max(s, axis=0, keepdims=True))
            alpha = jnp.exp2(m_prev - m_new)
            p = jnp.exp2(s - m_new)
            acc_sc[h] = alpha * acc_sc[h] + jnp.dot(v_ref[0, ki, h * MLA_VA:(h + 1) * MLA_VA, :], p.astype(BF16),
                                                    preferred_element_type=F32)
            m_sc[h] = m_new

    def body(ki, carry):
        block(ki, masked=False)
        return carry

    lax.fori_loop(0, qi, body, 0)
    block(qi, masked=True)
    o_t = jnp.concatenate([acc_sc[h, :MLA_V, :] / acc_sc[h, MLA_V:MLA_V + 1, :] for h in range(MLA_HPS)], axis=0)
    o_ref[0] = (o_t.T * _silu(z_ref[0].astype(F32))).astype(BF16)


def _mla_attn(qt, k, vt, z, tq):
    b, s, _ = k.shape
    nt = s // tq
    pairs = MLA_HEADS // MLA_HPS
    return pl.pallas_call(
        functools.partial(_mla_attn_kernel, tq=tq), grid=(b, pairs, nt),
        in_specs=[pl.BlockSpec((1, 1, MLA_HPS * LANES, tq), lambda i, p, j: (i, j, p, 0)),
                  pl.BlockSpec((1, s, MLA_HPS * LANES), lambda i, p, j: (i, 0, p)),
                  pl.BlockSpec((1, nt, MLA_HPS * MLA_VA, tq), lambda i, p, j: (i, 0, p, 0)),
                  pl.BlockSpec((1, tq, MLA_HPS * MLA_V), lambda i, p, j: (i, j, p))],
        out_specs=pl.BlockSpec((1, tq, MLA_HPS * MLA_V), lambda i, p, j: (i, j, p)),
        out_shape=jax.ShapeDtypeStruct((b, s, MLA_WIDTH), BF16),
        scratch_shapes=[pltpu.VMEM((MLA_HPS, 1, tq), F32), pltpu.VMEM((MLA_HPS, MLA_VA, tq), F32)],
        compiler_params=_cparams(("parallel", "parallel", "arbitrary")), name="mla_attn",
    )(qt, k, vt, z)


def _s5_tables(a_re, a_im, log_dt, b_re, b_im, c_re, c_im, n_steps):
    hp = lax.Precision.HIGHEST
    dt = jnp.exp(log_dt)[:, None]
    mag = jnp.exp(a_re * dt)
    ab_re, ab_im = mag * jnp.cos(a_im * dt), mag * jnp.sin(a_im * dt)
    den = a_re * a_re + a_im * a_im
    n_re, n_im = ab_re - 1.0, ab_im
    f_re = (n_re * a_re + n_im * a_im) / den
    f_im = (n_im * a_re - n_re * a_im) / den
    bb_re = f_re[..., None] * b_re - f_im[..., None] * b_im
    bb_im = f_re[..., None] * b_im + f_im[..., None] * b_re

    def lam_pow(k):
        k = k.astype(F32)[:, None, None]
        m = jnp.exp(k * a_re * dt)
        return m * jnp.cos(k * a_im * dt), m * jnp.sin(k * a_im * dt)

    L, C, P, G = S5_L, S5_GROUP, S5_STATE, S5_GROUPS
    lr, li = lam_pow(jnp.arange(L + 1))
    w_re = lr[..., None] * bb_re[None] - li[..., None] * bb_im[None]
    w_im = lr[..., None] * bb_im[None] + li[..., None] * bb_re[None]
    kk = (jnp.einsum("gdp,kgpc->kgdc", c_re, w_re[:L], precision=hp)
          - jnp.einsum("gdp,kgpc->kgdc", c_im, w_im[:L], precision=hp))
    lag = jnp.arange(L)[None, :] - jnp.arange(L)[:, None]
    kfull = jnp.where((lag >= 0)[:, :, None, None, None], kk[jnp.clip(lag, 0, L - 1)], 0.0)
    mt = jnp.transpose(kfull, (2, 0, 4, 1, 3)).reshape(G, L * C, L * C)
    e_re = jnp.transpose(w_re[:L][::-1], (1, 0, 3, 2)).reshape(G, L * C, P)
    e_im = jnp.transpose(w_im[:L][::-1], (1, 0, 3, 2)).reshape(G, L * C, P)
    w1 = jnp.concatenate([mt, e_re, e_im, -e_im, e_re], axis=-1)
    g_re = c_re[None] * lr[1:, :, None, :] - c_im[None] * li[1:, :, None, :]
    g_im = c_re[None] * li[1:, :, None, :] + c_im[None] * lr[1:, :, None, :]
    f_mat = jnp.concatenate([jnp.transpose(g_re, (1, 3, 0, 2)).reshape(G, P, L * C),
                             -jnp.transpose(g_im, (1, 3, 0, 2)).reshape(G, P, L * C)], axis=1)
    sr, si = lam_pow(L * (2 ** jnp.arange(n_steps)))
    la = jnp.transpose(jnp.concatenate([sr, sr], -1), (1, 0, 2))[:, :, None, :]
    lb = jnp.transpose(jnp.concatenate([si, si], -1), (1, 0, 2))[:, :, None, :]
    return w1.astype(BF16), f_mat.astype(BF16), la, lb


def _s5_kernel(u_ref, w1_ref, f_ref, la_ref, lb_ref, d_ref, y_ref, *, gb, n_steps):
    nc = u_ref.shape[2]
    row = lax.broadcasted_iota(jnp.int32, (nc, 2 * S5_STATE), 0)

    def shift(t, d):
        return jnp.where(row >= d, pltpu.roll(t, d, 0), 0.0)

    for g in range(gb):
        u = u_ref[0, g].reshape(nc, S5_FLAT)
        r = jnp.dot(u, w1_ref[g], preferred_element_type=F32)
        y = r[:, :S5_FLAT]
        x = r[:, S5_FLAT:S5_FLAT + 2 * S5_STATE]
        xs = r[:, S5_FLAT + 2 * S5_STATE:]
        for s in range(n_steps):
            a, b = la_ref[g, s], lb_ref[g, s]
            tx, txs = a * x + b * xs, a * xs - b * x
            x, xs = x + shift(tx, 1 << s), xs + shift(txs, 1 << s)
        h_in = shift(x, 1).astype(BF16)
        y = y + jnp.dot(h_in, f_ref[g], preferred_element_type=F32) + d_ref[g] * u.astype(F32)
        y_ref[0, g] = jax.nn.gelu(y, approximate=True).reshape(nc, S5_L, S5_GROUP).astype(BF16)


def _s5(u, tables, d, gb):
    w1, f_mat, la, lb = tables
    b, s, _ = u.shape
    nc = s // S5_L
    n_steps = la.shape[1]
    G = S5_GROUPS
    uf = u.reshape(b, nc, S5_L, G, S5_GROUP).transpose(0, 3, 1, 2, 4)
    d_flat = jnp.tile(d.reshape(G, 1, S5_GROUP), (1, 1, S5_L))
    grp = lambda i, j: (j, 0, 0)
    y = pl.pallas_call(
        functools.partial(_s5_kernel, gb=gb, n_steps=n_steps), grid=(b, G // gb),
        in_specs=[pl.BlockSpec((1, gb, nc, S5_L, S5_GROUP), lambda i, j: (i, j, 0, 0, 0)),
                  pl.BlockSpec((gb, S5_FLAT, 2 * S5_FLAT), grp),
                  pl.BlockSpec((gb, 2 * S5_STATE, S5_FLAT), grp),
                  pl.BlockSpec((gb, n_steps, 1, 2 * S5_STATE), lambda i, j: (j, 0, 0, 0)),
                  pl.BlockSpec((gb, n_steps, 1, 2 * S5_STATE), lambda i, j: (j, 0, 0, 0)),
                  pl.BlockSpec((gb, 1, S5_FLAT), grp)],
        out_specs=pl.BlockSpec((1, gb, nc, S5_L, S5_GROUP), lambda i, j: (i, j, 0, 0, 0)),
        out_shape=jax.ShapeDtypeStruct((b, G, nc, S5_L, S5_GROUP), BF16),
        compiler_params=_cparams(("parallel", "parallel")), name="s5",
    )(uf, w1, f_mat, la, lb, d_flat)
    return y.transpose(0, 2, 3, 1, 4).reshape(b, s, S5_WIDTH)


def _split_dot(x, e):
    hi = x.astype(BF16)
    lo = (x - hi.astype(F32)).astype(BF16)
    return jnp.dot(hi, e, preferred_element_type=F32) + jnp.dot(lo, e, preferred_element_type=F32)


def _ssd_kernel(xc_ref, dt_ref, dtb_ref, alog_ref, dexp_ref, e_ref, y_ref, state):
    L, PAD = M2_CHUNK, 8

    @pl.when(pl.program_id(1) == 0)
    def _():
        state[...] = jnp.zeros_like(state)

    xc = xc_ref[0].astype(F32)
    xs = xc[:, :M2_INNER]
    nbc = M2_GROUPS * M2_STATE
    bm = xc[:, M2_INNER:M2_INNER + nbc]
    cm = xc[:, M2_INNER + nbc:]

    dt = jax.nn.softplus(dt_ref[0] + dtb_ref[...])
    da = dt * (-jnp.exp(alog_ref[...]))
    ti = lax.broadcasted_iota(jnp.int32, (L, L), 0)
    si = lax.broadcasted_iota(jnp.int32, (L, L), 1)
    causal = si <= ti
    tril =jnp.where(causal, 1.0, 0.0).astype(BF16)
    hi = da.astype(BF16)
    r1 = da - hi.astype(F32)
    mid = r1.astype(BF16)
    lo = (r1 - mid.astype(F32)).astype(BF16)
    cs = (jnp.dot(tril, hi, preferred_element_type=F32) + jnp.dot(tril, mid, preferred_element_type=F32)
          + jnp.dot(tril, lo, preferred_element_type=F32))
    cs2 = cs * math.log2(math.e)
    cs2_t = cs2.T
    cs_end = cs[L - 1:L, :]
    stack = jnp.concatenate([dt, dt * jnp.exp(cs_end - cs), jnp.exp(cs)], axis=0).astype(BF16)
    fac = jnp.dot(stack, e_ref[...], preferred_element_type=F32)
    dt_e, dw_e, ecs_e = fac[:L], fac[L:2 * L], fac[2 * L:]
    dend_e = _split_dot(jnp.broadcast_to(jnp.exp(cs_end), (PAD, LANES)), e_ref[...])[0:1]
    x_dt = (xs * dt_e).astype(BF16)
    x_w = (xs * dw_e).astype(BF16)

    lane = lax.broadcasted_iota(jnp.int32, (L, LANES), 1)
    zero = jnp.zeros((L, LANES), BF16)
    for g in range(M2_GROUPS):
        gs = slice(g * M2_GW, (g + 1) * M2_GW)
        b_g = bm[:, g * M2_STATE:(g + 1) * M2_STATE]
        c_g = cm[:, g * M2_STATE:(g + 1) * M2_STATE].astype(BF16)
        cb = lax.dot_general(c_g, b_g.astype(BF16), (((1,), (1,)), ((), ())), preferred_element_type=F32)
        s_old = state[g]
        y_off = jnp.dot(c_g, s_old.astype(BF16), preferred_element_type=F32) * ecs_e[:, gs]
        state[g] = s_old * dend_e[:, gs] + jnp.dot(b_g.T.astype(BF16), x_w[:, gs], preferred_element_type=F32)
        for pr in range(M2_GW // LANES):
            ps = slice(g * M2_GW + pr * LANES, g * M2_GW + (pr + 1) * LANES)
            xp = x_dt[:, ps]
            y_pair = None
            for hh in range(2):
                h = (g * M2_GW + pr * LANES) // M2_HEADDIM + hh
                seg = jnp.exp2(jnp.where(causal, cs2[:, h:h + 1] - cs2_t[h:h + 1, :], -jnp.inf))
                att = (cb * seg).astype(BF16)
                x_h = jnp.where((lane < M2_HEADDIM) == (hh == 0), xp, zero)
                part = jnp.dot(att, x_h, preferred_element_type=F32)
                y_pair = part if y_pair is None else y_pair + part
            y = y_pair + y_off[:, pr * LANES:(pr + 1) * LANES] + dexp_ref[:, ps] * xs[:, ps]
            y_ref[0, :, ps] = y.astype(y_ref.dtype)


def _ssd(xc, dt_raw, dt_bias, a_log, d):
    b, s, _ = xc.shape
    pad_row = lambda v: jnp.pad(v, (0, LANES - M2_HEADS)).reshape(1, LANES)
    expand = (jnp.arange(LANES)[:, None] == (jnp.arange(M2_INNER) // M2_HEADDIM)[None, :]).astype(BF16)
    tok = lambda w: pl.BlockSpec((1, M2_CHUNK, w), lambda i, j: (i, j, 0))
    fixed = lambda shape: pl.BlockSpec(shape, lambda i, j: (0, 0))
    return pl.pallas_call(
        _ssd_kernel, grid=(b, s // M2_CHUNK),
        in_specs=[tok(M2_CONV_DIM), tok(LANES), fixed((1, LANES)), fixed((1, LANES)), fixed((1, M2_INNER)),
                  fixed((LANES, M2_INNER))],
        out_specs=tok(M2_INNER), out_shape=jax.ShapeDtypeStruct((b, s, M2_INNER), BF16),
        scratch_shapes=[pltpu.VMEM((M2_GROUPS, M2_STATE, M2_GW), F32)],
        compiler_params=_cparams(("parallel", "arbitrary")), name="ssd",
    )(xc, dt_raw, pad_row(dt_bias), pad_row(a_log), jnp.repeat(d, M2_HEADDIM).reshape(1, M2_INNER), expand)


def _cols(w, sizes):
    idx, out = 0, []
    for n in sizes:
        out.append(w[:, idx:idx + n])
        idx += n
    return out


def _even_layer(h, b, s, i, layer, p, mem_k, mem_v, rope_t, tm, tq):
    w_u, w_za, w_cq, w_ckv, w_kr, w_zb, w_qm, w_zm = _cols(
        p["ev_w_in"][i], (S5_WIDTH, S5_WIDTH, MLA_Q_LORA, MLA_KV_LORA, MLA_ROPE, MLA_WIDTH, MEM_WIDTH, MEM_WIDTH))
    w_kr = jnp.pad(w_kr, ((0, 0), (MLA_NOPE, LANES - MLA_QK)))
    weights = [w.astype(BF16) for w in (w_u, w_za, w_zb, w_qm, w_zm, w_cq, w_ckv, w_kr)]
    u, za, zb, qm, zm, cq, ckv, kr = _rms_proj(h, p["norm_g"][layer], weights, [BF16] * 5 + [F32] * 3, tm)

    n_steps = max(1, (s // S5_L - 1).bit_length())
    tables = _s5_tables(p["s5_a_re"][i], p["s5_a_im"][i], p["s5_log_dt"][i], p["s5_b_re"][i], p["s5_b_im"][i],
                        p["s5_c_re"][i], p["s5_c_im"][i], n_steps)
    y_s5 = _s5(u.reshape(b, s, S5_WIDTH), tables, p["s5_d"][i], gb=8).reshape(b * s, S5_WIDTH)

    qt, k, vt = _mla_prep(cq, ckv, kr, rope_t, p["mla_q_a_norm_g"][i], p["mla_w_uq"][i], p["mla_kv_a_norm_g"][i],
                          p["mla_w_ukv"][i], p["mla_q_norm_g"][i], p["mla_k_norm_g"][i], b, s, tq)
    y_b = _mla_attn(qt, k.reshape(b, s, MLA_HEADS * LANES), vt, zb.reshape(b, s, MLA_WIDTH),
                    tq).reshape(b * s, MLA_WIDTH)

    y_m = _mem_attn(qm.reshape(b, s, MEM_WIDTH), zm.reshape(b, s, MEM_WIDTH), mem_k, mem_v,
                    p["mem_q_norm_g"][layer], layer, 2 * tm if s % (2 * tm) == 0 else tm).reshape(b * s, MEM_WIDTH)

    w_out = p["ev_w_out"][i].astype(BF16)
    ws = [w_out[:S5_WIDTH], w_out[S5_WIDTH:S5_WIDTH + MLA_WIDTH], w_out[S5_WIDTH + MLA_WIDTH:]]
    glu = (za, p["s5_glu_w"][i].astype(BF16), p["s5_glu_b"][i].reshape(1, S5_WIDTH))
    return _out_proj(h, [y_s5, y_b, y_m], ws, tm, glu)


def _odd_layer(h, b, s, i, layer, p, mem_k, mem_v, tm):
    w_z, w_xbc, w_dt, w_qm, w_zm = _cols(p["od_w_in"][i], (M2_INNER, M2_CONV_DIM, M2_HEADS, MEM_WIDTH, MEM_WIDTH))
    w_dt = jnp.pad(w_dt, ((0, 0), (0, LANES - M2_HEADS)))
    weights = [w.astype(BF16) for w in (w_z, w_xbc, w_qm, w_zm, w_dt)]
    z, xc, qm, zm, dt_raw = _rms_proj(h, p["norm_g"][layer], weights, [BF16] * 4 + [F32], tm,
                                      conv=(1, p["m2_conv_w"][i], p["m2_conv_b"][i]), seq_len=s)
    y_c = _ssd(xc.reshape(b, s, M2_CONV_DIM), dt_raw.reshape(b, s, LANES), p["m2_dt_bias"][i], p["m2_a_log"][i],
               p["m2_d"][i]).reshape(b * s, M2_INNER)
    y_m = _mem_attn(qm.reshape(b, s, MEM_WIDTH), zm.reshape(b, s, MEM_WIDTH), mem_k, mem_v,
                    p["mem_q_norm_g"][layer], layer, 2 * tm if s % (2 * tm) == 0 else tm).reshape(b * s, MEM_WIDTH)
    w_out = p["od_w_out"][i].astype(BF16)
    return _out_proj(h, [y_c, y_m], [w_out[:M2_INNER], w_out[M2_INNER:]], tm,
                     m2_args=(z, p["m2_norm_g"][i].reshape(1, M2_INNER)))


def _token_tile(s):
    return 512 if s % 512 == 0 else s


def kernel(x, mem, positions, norm_g, mem_norm_g, mem_w_kv, mem_q_norm_g, mem_k_norm_g, ev_w_in, ev_w_out, s5_a_re, s5_a_im, s5_log_dt, s5_b_re, s5_b_im, s5_c_re, s5_c_im, s5_d, s5_glu_w, s5_glu_b, mla_q_a_norm_g, mla_w_uq, mla_kv_a_norm_g, mla_w_ukv, mla_q_norm_g, mla_k_norm_g, od_w_in, od_w_out, m2_conv_w, m2_conv_b, m2_dt_bias, m2_a_log, m2_d, m2_norm_g):
    p = dict(norm_g=norm_g, mem_q_norm_g=mem_q_norm_g, ev_w_in=ev_w_in, ev_w_out=ev_w_out,
             s5_a_re=s5_a_re, s5_a_im=s5_a_im, s5_log_dt=s5_log_dt, s5_b_re=s5_b_re, s5_b_im=s5_b_im,
             s5_c_re=s5_c_re, s5_c_im=s5_c_im, s5_d=s5_d, s5_glu_w=s5_glu_w, s5_glu_b=s5_glu_b,
             mla_q_a_norm_g=mla_q_a_norm_g, mla_w_uq=mla_w_uq, mla_kv_a_norm_g=mla_kv_a_norm_g,
             mla_w_ukv=mla_w_ukv, mla_q_norm_g=mla_q_norm_g, mla_k_norm_g=mla_k_norm_g,
             od_w_in=od_w_in, od_w_out=od_w_out, m2_conv_w=m2_conv_w, m2_conv_b=m2_conv_b,
             m2_dt_bias=m2_dt_bias, m2_a_log=m2_a_log, m2_d=m2_d, m2_norm_g=m2_norm_g)
    b, s, d = x.shape
    tm = _token_tile(s)
    tq = 1024 if s % 1024 == 0 else s
    mem_k, mem_v = _mem_kv(mem, mem_norm_g, mem_w_kv, mem_k_norm_g)
    rope_t = _rope_tables(positions)
    h = x.reshape(b * s, d)
    for layer in range(DEPTH):
        if layer % 2 == 0:
            h = _even_layer(h, b, s, layer // 2, layer, p, mem_k, mem_v, rope_t, tm, tq)
        else:
            h = _odd_layer(h, b, s, layer // 2, layer, p, mem_k, mem_v, tm)
    return h.reshape(b, s, d)
```

```python
import functools
import math

import jax
import jax.numpy as jnp
from jax import lax
from jax.experimental import pallas as pl
from jax.experimental.pallas import tpu as pltpu

F32 = jnp.float32
BF16 = jnp.bfloat16

D_MODEL = 1024
DEPTH = 4
CHUNK = 64
N_MEM = 256
RMS_EPS = 1e-6

S5_WIDTH = 512
S5_GROUP = 16
S5_GROUPS = S5_WIDTH // S5_GROUP
S5_STATE = 64
S5_L = 16
S5_FLAT = S5_L * S5_GROUP

MLA_HEADS = 8
MLA_NOPE = 64
MLA_ROPE = 32
MLA_QK = MLA_NOPE + MLA_ROPE
MLA_V = 64
MLA_WIDTH = MLA_HEADS * MLA_V
MLA_HPS = 2
MLA_VA = MLA_V + 16
MLA_Q_LORA = 256
MLA_KV_LORA = 128
ROPE_BASE = 10000.0
ROPE_HALF = MLA_ROPE // 2
LANES = 128

M2_INNER = 2 * D_MODEL
M2_HEADDIM = 64
M2_HEADS = M2_INNER // M2_HEADDIM
M2_GROUPS = 4
M2_STATE = 128
M2_CONV = 4
M2_CHUNK = 128
M2_CONV_DIM = M2_INNER + 2 * M2_GROUPS * M2_STATE
M2_GW = M2_INNER // M2_GROUPS

MEM_HEADS = 4
MEM_HD = 128
MEM_WIDTH = MEM_HEADS * MEM_HD

VMEM_LIMIT = 56 * 1024 * 1024


def _cparams(sem):
    return pltpu.CompilerParams(dimension_semantics=sem, vmem_limit_bytes=VMEM_LIMIT)


def _silu(z):
    h = 0.5 * z
    return h + h * jnp.tanh(h)


CONV_PAD = 8


def _rms_proj_kernel(h_ref, g_ref, *refs, n_out, col_chunk, conv_idx, tiles_per_seq):
    w_refs = refs[:n_out]
    if conv_idx is None:
        o_refs = refs[n_out:]
    else:
        cw_ref, cb_ref = refs[n_out:n_out + 2]
        o_refs = refs[n_out + 2:2 * n_out + 2]
        pad, tail = refs[2 * n_out + 2:]

        @pl.when(pl.program_id(0) % tiles_per_seq == 0)
        def _():
            tail[...] = jnp.zeros_like(tail)

    x = h_ref[...]
    tm = x.shape[0]
    xn = (x * lax.rsqrt(jnp.mean(x * x, axis=-1, keepdims=True) + RMS_EPS) * g_ref[...]).astype(BF16)
    for i, (w_ref, o_ref) in enumerate(zip(w_refs, o_refs)):
        n = w_ref.shape[1]
        for c0 in range(0, n, col_chunk):
            c1 = min(n, c0 + col_chunk)
            r = jnp.dot(xn, w_ref[:, c0:c1], preferred_element_type=F32)
            if i == conv_idx:
                pad[0:CONV_PAD, :] = tail[:, c0:c1]
                pad[CONV_PAD:CONV_PAD + tm, :] = r
                first = CONV_PAD - (M2_CONV - 1)
                acc = cb_ref[:, c0:c1] + cw_ref[0:1, c0:c1] * pad[first:first + tm, :]
                for k in range(1, M2_CONV):
                    acc = acc + cw_ref[k:k + 1, c0:c1] * pad[first + k:first + k + tm, :]
                tail[:, c0:c1] = pad[tm:tm + CONV_PAD, :]
                r = _silu(acc)
            o_ref[:, c0:c1] = r.astype(o_ref.dtype)


def _rms_proj(h, g, weights, out_dtypes, tm, conv=None, seq_len=None):
    t, d = h.shape
    col_chunk = 256 if conv is not None else 512
    fixed = lambda i: (0, 0)
    in_specs = [pl.BlockSpec((tm, d), lambda i: (i, 0)), pl.BlockSpec((1, d), fixed)]
    in_specs += [pl.BlockSpec(w.shape, fixed) for w in weights]
    args = [h, g.reshape(1, d), *weights]
    scratch, conv_idx, tiles_per_seq = [], None, None
    if conv is not None:
        conv_idx, cw, cb = conv
        n = weights[conv_idx].shape[1]
        in_specs += [pl.BlockSpec(cw.shape, fixed), pl.BlockSpec((1, n), fixed)]
        args += [cw, cb.reshape(1, n)]
        scratch = [pltpu.VMEM((tm + CONV_PAD, col_chunk), F32), pltpu.VMEM((CONV_PAD, n), F32)]
        tiles_per_seq = seq_len // tm
    out_specs = [pl.BlockSpec((tm, w.shape[1]), lambda i: (i, 0)) for w in weights]
    out_shape = [jax.ShapeDtypeStruct((t, w.shape[1]), dt) for w, dt in zip(weights, out_dtypes)]
    return pl.pallas_call(
        functools.partial(_rms_proj_kernel, n_out=len(weights), col_chunk=col_chunk, conv_idx=conv_idx,
                          tiles_per_seq=tiles_per_seq),
        grid=(t // tm,), in_specs=in_specs, out_specs=out_specs, out_shape=out_shape, scratch_shapes=scratch,
        compiler_params=_cparams(("arbitrary",)), name="rms_proj",
    )(*args)


def _out_proj_kernel(h_ref, *refs, n_in, pre):
    x_refs, w_refs = refs[:n_in], refs[n_in:2 * n_in]
    extra = refs[2 * n_in:-1]
    o_ref = refs[-1]
    acc = h_ref[...]
    for i, (x_ref, w_ref) in enumerate(zip(x_refs, w_refs)):
        if i == 0 and pre == "m2":
            z_ref, ng_ref = extra
            for g in range(M2_GROUPS):
                gs = slice(g * M2_GW, (g + 1) * M2_GW)
                gated = x_ref[:, gs].astype(F32) * _silu(z_ref[:, gs].astype(F32))
                xn = gated * lax.rsqrt(jnp.mean(gated * gated, axis=-1, keepdims=True) + RMS_EPS) * ng_ref[:, gs]
                acc = acc + jnp.dot(xn.astype(BF16), w_ref[gs, :], preferred_element_type=F32)
            continue
        x = x_ref[...]
        if i == 0 and pre == "glu":
            za_ref, gw_ref, gb_ref = extra
            gate = jnp.dot(x, gw_ref[...], preferred_element_type=F32) + gb_ref[...]
            x = (x.astype(F32) * jax.nn.sigmoid(gate) * _silu(za_ref[...].astype(F32))).astype(BF16)
        acc = acc + jnp.dot(x, w_ref[...], preferred_element_type=F32)
    o_ref[...] = acc


def _out_proj(h, xs, ws, tm, glu_args=None, m2_args=None):
    t, d = h.shape
    row = lambda i: (i, 0)
    fixed = lambda i: (0, 0)
    in_specs = [pl.BlockSpec((tm, d), row)]
    in_specs += [pl.BlockSpec((tm, x.shape[1]), row) for x in xs]
    in_specs += [pl.BlockSpec(w.shape, fixed) for w in ws]
    args = [h, *xs, *ws]
    if glu_args is not None:
        za, gw, gb = glu_args
        in_specs += [pl.BlockSpec((tm, za.shape[1]), row), pl.BlockSpec(gw.shape, fixed),
                     pl.BlockSpec(gb.shape, fixed)]
        args += [za, gw, gb]
    if m2_args is not None:
        z, ng = m2_args
        in_specs += [pl.BlockSpec((tm, z.shape[1]), row), pl.BlockSpec(ng.shape, fixed)]
        args += [z, ng]
    pre = "glu" if glu_args is not None else "m2" if m2_args is not None else None
    return pl.pallas_call(
        functools.partial(_out_proj_kernel, n_in=len(xs), pre=pre),
        grid=(t // tm,), in_specs=in_specs, out_specs=pl.BlockSpec((tm, d), row),
        out_shape=jax.ShapeDtypeStruct((t, d), F32),
        compiler_params=_cparams(("parallel",)), name="out_proj",
    )(*args)


def _mem_kv_kernel(mem_ref, g_ref, w_ref, kg_ref, k_ref, v_ref):
    x = mem_ref[0]
    xn = (x * lax.rsqrt(jnp.mean(x * x, axis=-1, keepdims=True) + RMS_EPS) * g_ref[0]).astype(BF16)
    kv = jnp.dot(xn, w_ref[0], preferred_element_type=F32)
    for h in range(MEM_HEADS):
        kh = kv[:, h * MEM_HD:(h + 1) * MEM_HD]
        kn = kh * lax.rsqrt(jnp.mean(kh * kh, axis=-1, keepdims=True) + RMS_EPS) * kg_ref[0]
        k_ref[0, 0, :, h * MEM_HD:(h + 1) * MEM_HD] = kn.astype(BF16)
    v_ref[0, 0] = kv[:, MEM_WIDTH:].astype(BF16)


def _mem_kv(mem, mem_norm_g, w_kv, k_norm_g):
    b = mem.shape[0]
    out = jax.ShapeDtypeStruct((DEPTH, b, N_MEM, MEM_WIDTH), BF16)
    return pl.pallas_call(
        _mem_kv_kernel, grid=(DEPTH, b),
        in_specs=[pl.BlockSpec((1, N_MEM, D_MODEL), lambda l, i: (i, 0, 0)),
                  pl.BlockSpec((1, 1, D_MODEL), lambda l, i: (l, 0, 0)),
                  pl.BlockSpec((1, D_MODEL, 2 * MEM_WIDTH), lambda l, i: (l, 0, 0)),
                  pl.BlockSpec((1, 1, MEM_HD), lambda l, i: (l, 0, 0))],
        out_specs=[pl.BlockSpec((1, 1, N_MEM, MEM_WIDTH), lambda l, i: (l, i, 0, 0))] * 2,
        out_shape=[out, out], compiler_params=_cparams(("arbitrary", "arbitrary")), name="mem_kv",
    )(mem, mem_norm_g.reshape(DEPTH, 1, D_MODEL), w_kv.astype(BF16), k_norm_g.reshape(DEPTH, 1, MEM_HD))


def _mem_attn_kernel(q_ref, z_ref, k_ref, v_ref, qg_ref, o_ref):
    scale = 1.0 / math.sqrt(MEM_HD)
    for h in range(MEM_HEADS):
        sl = slice(h * MEM_HD, (h + 1) * MEM_HD)
        q = q_ref[0, :, sl].astype(F32)
        qn = (q * lax.rsqrt(jnp.mean(q * q, axis=-1, keepdims=True) + RMS_EPS) * (qg_ref[...] * scale)).astype(BF16)
        s = lax.dot_general(qn, k_ref[0, 0, :, sl], (((1,), (1,)), ((), ())), preferred_element_type=F32)
        p = jnp.exp(s - jnp.max(s, axis=-1, keepdims=True))
        l = jnp.sum(p, axis=-1, keepdims=True)
        o = jnp.dot(p.astype(BF16), v_ref[0, 0, :, sl], preferred_element_type=F32) / l
        o_ref[0, :, sl] = (o * _silu(z_ref[0, :, sl].astype(F32))).astype(BF16)


def _mem_attn(q, z, k_all, v_all, qg, layer, tq):
    b, s, _ = q.shape
    tok = pl.BlockSpec((1, tq, MEM_WIDTH), lambda i, j: (i, j, 0))
    bank = pl.BlockSpec((1, 1, N_MEM, MEM_WIDTH), lambda i, j: (layer, i, 0, 0))
    return pl.pallas_call(
        _mem_attn_kernel, grid=(b, s // tq),
        in_specs=[tok, tok, bank, bank, pl.BlockSpec((1, MEM_HD), lambda i, j: (0, 0))],
        out_specs=tok, out_shape=jax.ShapeDtypeStruct((b, s, MEM_WIDTH), BF16),
        compiler_params=_cparams(("parallel", "parallel")), name="mem_attn",
    )(q, z, k_all, v_all, qg.reshape(1, MEM_HD))


def _rope_table_kernel(pos_ref, inv_ref, cos_ref, sin_ref):
    ang = pos_ref[0].astype(F32) * inv_ref[...]
    cos_ref[0] = jnp.cos(ang)
    sin_ref[0] = jnp.sin(ang)


def _rope_tables(positions):
    b, s = positions.shape
    inv = ROPE_BASE ** (-jnp.arange(ROPE_HALF, dtype=F32) / ROPE_HALF)
    out = jax.ShapeDtypeStruct((b, ROPE_HALF, s), F32)
    cos, sin = pl.pallas_call(
        _rope_table_kernel, grid=(b,),
        in_specs=[pl.BlockSpec((1, 1, s), lambda i: (i, 0, 0)), pl.BlockSpec((ROPE_HALF, 1), lambda i: (0, 0))],
        out_specs=[pl.BlockSpec((1, ROPE_HALF, s), lambda i: (i, 0, 0))] * 2,
        out_shape=[out, out], compiler_params=_cparams(("parallel",)), name="rope_tables",
    )(positions.reshape(b, 1, s), inv.reshape(ROPE_HALF, 1))
    tail = LANES - MLA_QK
    cos_q = jnp.concatenate([cos, cos], 1)
    sin_q = jnp.concatenate([-sin, sin], 1)
    cos_k = jnp.concatenate([jnp.ones((b, s, MLA_NOPE), F32), jnp.swapaxes(cos_q, 1, 2), jnp.ones((b, s, tail), F32)], -1)
    sin_k = jnp.concatenate([jnp.zeros((b, s, MLA_NOPE), F32), jnp.swapaxes(sin_q, 1, 2), jnp.zeros((b, s, tail), F32)], -1)
    return cos_q, sin_q, cos_k.reshape(b * s, LANES), sin_k.reshape(b * s, LANES)


def _head_norm_rope(x, gain, cos, sin, lane):
    xn = x * lax.rsqrt(jnp.sum(x * x, axis=-1, keepdims=True) * (1.0 / MLA_QK) + RMS_EPS) * gain
    partner = jnp.where(lane < MLA_NOPE + ROPE_HALF,
                        pltpu.roll(xn, LANES - ROPE_HALF, 1), pltpu.roll(xn, ROPE_HALF, 1))
    return xn * cos + partner * sin


def _mla_prep_kernel(cq_ref, ckv_ref, kr_ref, cosq_ref, sinq_ref, cosk_ref, sink_ref, gqa_ref, wqt_ref, gkva_ref,
                     wk_ref, wvt_ref, gq_ref, gk_ref, qt_ref, k_ref, vt_ref):
    cq = cq_ref[...]
    cqn = cq * lax.rsqrt(jnp.mean(cq * cq, axis=-1, keepdims=True) + RMS_EPS) * gqa_ref[...]
    ckv = ckv_ref[...]
    ckvn = ckv * lax.rsqrt(jnp.mean(ckv * ckv, axis=-1, keepdims=True) + RMS_EPS) * gkva_ref[...]
    cqn_t = cqn.T.astype(BF16)
    ckvn_t = ckvn.T.astype(BF16)
    ckvn = ckvn.astype(BF16)
    kr = kr_ref[...]
    cosq, sinq, cosk, sink = cosq_ref[0], sinq_ref[0], cosk_ref[...], sink_ref[...]
    lane = lax.broadcasted_iota(jnp.int32, cosk.shape, 1)
    qscale = math.log2(math.e) / math.sqrt(MLA_QK)
    tm = cq.shape[0]
    ones_tile = jnp.where(lax.broadcasted_iota(jnp.int32, (MLA_VA - MLA_V, tm), 0) == 0, 1.0, 0.0).astype(BF16)
    for h in range(MLA_HEADS):
        sl = slice(h * LANES, (h + 1) * LANES)
        qt = jnp.dot(wqt_ref[sl, :], cqn_t, preferred_element_type=F32)
        qn = qt * lax.rsqrt(jnp.sum(qt * qt, axis=0, keepdims=True) * (1.0 / MLA_QK) + RMS_EPS) * gq_ref[...]
        lo, hi = h * LANES + MLA_NOPE, h * LANES + MLA_QK
        partner = jnp.concatenate([qn[MLA_NOPE + ROPE_HALF:MLA_QK], qn[MLA_NOPE:MLA_NOPE + ROPE_HALF]], axis=0)
        qt_ref[0, 0, h * LANES:lo, :] = (qn[:MLA_NOPE] * qscale).astype(BF16)
        qt_ref[0, 0, lo:hi, :] = ((qn[MLA_NOPE:MLA_QK] * cosq + partner * sinq) * qscale).astype(BF16)
        qt_ref[0, 0, hi:(h + 1) * LANES, :] = jnp.zeros((LANES - MLA_QK, tm), BF16)
        kh = jnp.dot(ckvn, wk_ref[:, sl], preferred_element_type=F32) + kr
        k_ref[:, sl] = _head_norm_rope(kh, gk_ref[...], cosk, sink, lane).astype(BF16)
        vs = slice(h * MLA_V, (h + 1) * MLA_V)
        vt_ref[0, 0, h * MLA_VA:h * MLA_VA + MLA_V, :] = jnp.dot(wvt_ref[vs, :], ckvn_t,
                                                                 preferred_element_type=F32).astype(BF16)
        vt_ref[0, 0, h * MLA_VA + MLA_V:(h + 1) * MLA_VA, :] = ones_tile


def _pad_heads(w, n_heads, width, offset=0):
    k = w.shape[0]
    w = w.reshape(k, n_heads, width)
    w = jnp.pad(w, ((0, 0), (0, 0), (offset, LANES - width - offset)))
    return w.reshape(k, n_heads * LANES)


def _mla_prep(cq, ckv, kr, rope_t, gqa, w_uq, gkva, w_ukv, gq, gk, b, s, tm):
    cos_q, sin_q, cos_k, sin_k = rope_t
    nt = s // tm
    wqt = _pad_heads(w_uq, MLA_HEADS, MLA_QK).T.astype(BF16)
    w_ukv = w_ukv.reshape(MLA_KV_LORA, MLA_HEADS, MLA_NOPE + MLA_V)
    wk = _pad_heads(w_ukv[:, :, :MLA_NOPE].reshape(MLA_KV_LORA, -1), MLA_HEADS, MLA_NOPE).astype(BF16)
    wvt = w_ukv[:, :, MLA_NOPE:].reshape(MLA_KV_LORA, MLA_WIDTH).T.astype(BF16)
    pad_gain = lambda g: jnp.pad(g, (0, LANES - MLA_QK))
    row = lambda i, j: (i * nt + j, 0)
    fixed = lambda i, j: (0, 0)
    hw = MLA_HEADS * LANES
    return pl.pallas_call(
        _mla_prep_kernel, grid=(b, nt),
        in_specs=[pl.BlockSpec((tm, MLA_Q_LORA), row), pl.BlockSpec((tm, MLA_KV_LORA), row),
                  pl.BlockSpec((tm, LANES), row),
                  pl.BlockSpec((1, MLA_ROPE, tm), lambda i, j: (i, 0, j)),
                  pl.BlockSpec((1, MLA_ROPE, tm), lambda i, j: (i, 0, j)),
                  pl.BlockSpec((tm, LANES), row), pl.BlockSpec((tm, LANES), row),
                  pl.BlockSpec((1, MLA_Q_LORA), fixed), pl.BlockSpec((hw, MLA_Q_LORA), fixed),
                  pl.BlockSpec((1, MLA_KV_LORA), fixed), pl.BlockSpec((MLA_KV_LORA, hw), fixed),
                  pl.BlockSpec((MLA_WIDTH, MLA_KV_LORA), fixed),
                  pl.BlockSpec((LANES, 1), fixed), pl.BlockSpec((1, LANES), fixed)],
        out_specs=[pl.BlockSpec((1, 1, hw, tm), lambda i, j: (i, j, 0, 0)), pl.BlockSpec((tm, hw), row),
                   pl.BlockSpec((1, 1, MLA_HEADS * MLA_VA, tm), lambda i, j: (i, j, 0, 0))],
        out_shape=[jax.ShapeDtypeStruct((b, nt, hw, tm), BF16), jax.ShapeDtypeStruct((b * s, hw), BF16),
                   jax.ShapeDtypeStruct((b, nt, MLA_HEADS * MLA_VA, tm), BF16)],
        compiler_params=_cparams(("parallel", "parallel")), name="mla_prep",
    )(cq, ckv, kr, cos_q, sin_q, cos_k, sin_k, gqa.reshape(1, -1), wqt, gkva.reshape(1, -1), wk, wvt,
      pad_gain(gq).reshape(LANES, 1), pad_gain(gk).reshape(1, LANES))


def _mla_attn_kernel(q_ref, k_ref, v_ref, z_ref, o_ref, m_sc, acc_sc, *, tq):
    qi = pl.program_id(2)
    m_sc[...] = jnp.full(m_sc.shape, -jnp.inf, F32)
    acc_sc[...] = jnp.zeros(acc_sc.shape, F32)

    def block(ki, masked):
        start = pl.multiple_of(ki * tq, tq)
        for h in range(MLA_HPS):
            s = jnp.dot(k_ref[0, pl.ds(start, tq), h * LANES:(h + 1) * LANES], q_ref[0, 0, h * LANES:(h + 1) * LANES, :],
                        preferred_element_type=F32)
            if masked:
                kc = lax.broadcasted_iota(jnp.int32, s.shape, 0) // CHUNK
                qc = lax.broadcasted_iota(jnp.int32, s.shape, 1) // CHUNK
                s = jnp.where(kc <= qc, s, -jnp.inf)
            m_prev = m_sc[h]
            m_new = jnp.maximum(m_prev, jnp.max(s, axis=0, keepdims=True))
            alpha = jnp.exp2(m_prev - m_new)
            p = jnp.exp2(s - m_new)
            acc_sc[h] = alpha * acc_sc[h] + jnp.dot(v_ref[0, ki, h * MLA_VA:(h + 1) * MLA_VA, :], p.astype(BF16),
                                                    preferred_element_type=F32)
            m_sc[h] = m_new

    def body(ki, carry):
        block(ki, masked=False)
        return carry

    lax.fori_loop(0, qi, body, 0)
    block(qi, masked=True)
    o_t = jnp.concatenate([acc_sc[h, :MLA_V, :] / acc_sc[h, MLA_V:MLA_V + 1, :] for h in range(MLA_HPS)], axis=0)
    o_ref[0] = (o_t.T * _silu(z_ref[0].astype(F32))).astype(BF16)


def _mla_attn(qt, k, vt, z, tq):
    b, s, _ = k.shape
    nt = s // tq
    pairs = MLA_HEADS // MLA_HPS
    return pl.pallas_call(
        functools.partial(_mla_attn_kernel, tq=tq), grid=(b, pairs, nt),
        in_specs=[pl.BlockSpec((1, 1, MLA_HPS * LANES, tq), lambda i, p, j: (i, j, p, 0)),
                  pl.BlockSpec((1, s, MLA_HPS * LANES), lambda i, p, j: (i, 0, p)),
                  pl.BlockSpec((1, nt, MLA_HPS * MLA_VA, tq), lambda i, p, j: (i, 0, p, 0)),
                  pl.BlockSpec((1, tq, MLA_HPS * MLA_V), lambda i, p, j: (i, j, p))],
        out_specs=pl.BlockSpec((1, tq, MLA_HPS * MLA_V), lambda i, p, j: (i, j, p)),
        out_shape=jax.ShapeDtypeStruct((b, s, MLA_WIDTH), BF16),
        scratch_shapes=[pltpu.VMEM((MLA_HPS, 1, tq), F32), pltpu.VMEM((MLA_HPS, MLA_VA, tq), F32)],
        compiler_params=_cparams(("parallel", "parallel", "arbitrary")), name="mla_attn",
    )(qt, k, vt, z)


def _s5_tables(a_re, a_im, log_dt, b_re, b_im, c_re, c_im, n_steps):
    hp = lax.Precision.HIGHEST
    dt = jnp.exp(log_dt)[:, None]
    mag = jnp.exp(a_re * dt)
    ab_re, ab_im = mag * jnp.cos(a_im * dt), mag * jnp.sin(a_im * dt)
    den = a_re * a_re + a_im * a_im
    n_re, n_im = ab_re - 1.0, ab_im
    f_re = (n_re * a_re + n_im * a_im) / den
    f_im = (n_im * a_re - n_re * a_im) / den
    bb_re = f_re[..., None] * b_re - f_im[..., None] * b_im
    bb_im = f_re[..., None] * b_im + f_im[..., None] * b_re

    def lam_pow(k):
        k = k.astype(F32)[:, None, None]
        m = jnp.exp(k * a_re * dt)
        return m * jnp.cos(k * a_im * dt), m * jnp.sin(k * a_im * dt)

    L, C, P, G = S5_L, S5_GROUP, S5_STATE, S5_GROUPS
    lr, li = lam_pow(jnp.arange(L + 1))
    w_re = lr[..., None] * bb_re[None] - li[..., None] * bb_im[None]
    w_im = lr[..., None] * bb_im[None] + li[..., None] * bb_re[None]
    kk = (jnp.einsum("gdp,kgpc->kgdc", c_re, w_re[:L], precision=hp)
          - jnp.einsum("gdp,kgpc->kgdc", c_im, w_im[:L], precision=hp))
    lag = jnp.arange(L)[None, :] - jnp.arange(L)[:, None]
    kfull = jnp.where((lag >= 0)[:, :, None, None, None], kk[jnp.clip(lag, 0, L - 1)], 0.0)
    mt = jnp.transpose(kfull, (2, 0, 4, 1, 3)).reshape(G, L * C, L * C)
    e_re = jnp.transpose(w_re[:L][::-1], (1, 0, 3, 2)).reshape(G, L * C, P)
    e_im = jnp.transpose(w_im[:L][::-1], (1, 0, 3, 2)).reshape(G, L * C, P)
    w1 = jnp.concatenate([mt, e_re, e_im, -e_im, e_re], axis=-1)
    g_re = c_re[None] * lr[1:, :, None, :] - c_im[None] * li[1:, :, None, :]
    g_im = c_re[None] * li[1:, :, None, :] + c_im[None] * lr[1:, :, None, :]
    f_mat = jnp.concatenate([jnp.transpose(g_re, (1, 3, 0, 2)).reshape(G, P, L * C),
                             -jnp.transpose(g_im, (1, 3, 0, 2)).reshape(G, P, L * C)], axis=1)
    sr, si = lam_pow(L * (2 ** jnp.arange(n_steps)))
    la = jnp.transpose(jnp.concatenate([sr, sr], -1), (1, 0, 2))[:, :, None, :]
    lb = jnp.transpose(jnp.concatenate([si, si], -1), (1, 0, 2))[:, :, None, :]
    return w1.astype(BF16), f_mat.astype(BF16), la, lb


def _s5_kernel(u_ref, w1_ref, f_ref, la_ref, lb_ref, d_ref, y_ref, *, gb, n_steps):
    nc = u_ref.shape[1]
    row = lax.broadcasted_iota(jnp.int32, (nc, 2 * S5_STATE), 0)

    def shift(t, d):
        return jnp.where(row >= d, pltpu.roll(t, d, 0), 0.0)

    for g in range(gb):
        u = u_ref[0, :, :, g * S5_GROUP:(g + 1) * S5_GROUP].reshape(nc, S5_FLAT)
        r = jnp.dot(u, w1_ref[g], preferred_element_type=F32)
        y = r[:, :S5_FLAT]
        x = r[:, S5_FLAT:S5_FLAT + 2 * S5_STATE]
        xs = r[:, S5_FLAT + 2 * S5_STATE:]
        for s in range(n_steps):
            a, b = la_ref[g, s], lb_ref[g, s]
            tx, txs = a * x + b * xs, a * xs - b * x
            x, xs = x + shift(tx, 1 << s), xs + shift(txs, 1 << s)
        h_in = shift(x, 1).astype(BF16)
        y = y + jnp.dot(h_in, f_ref[g], preferred_element_type=F32) + d_ref[g] * u.astype(F32)
        y_ref[0, g] = jax.nn.gelu(y, approximate=True).astype(BF16)


def _s5(u, tables, d, gb):
    w1, f_mat, la, lb = tables
    b, s, _ = u.shape
    nc = s // S5_L
    n_steps = la.shape[1]
    G = S5_GROUPS
    uf = u.reshape(b, nc, S5_L, S5_WIDTH)
    d_flat = jnp.tile(d.reshape(G, 1, S5_GROUP), (1, 1, S5_L))
    grp = lambda i, j: (j, 0, 0)
    y = pl.pallas_call(
        functools.partial(_s5_kernel, gb=gb, n_steps=n_steps), grid=(b, G // gb),
        in_specs=[pl.BlockSpec((1, nc, S5_L, gb * S5_GROUP), lambda i, j: (i, 0, 0, j)),
                  pl.BlockSpec((gb, S5_FLAT, 2 * S5_FLAT), grp),
                  pl.BlockSpec((gb, 2 * S5_STATE, S5_FLAT), grp),
                  pl.BlockSpec((gb, n_steps, 1, 2 * S5_STATE), lambda i, j: (j, 0, 0, 0)),
                  pl.BlockSpec((gb, n_steps, 1, 2 * S5_STATE), lambda i, j: (j, 0, 0, 0)),
                  pl.BlockSpec((gb, 1, S5_FLAT), grp)],
        out_specs=pl.BlockSpec((1, gb, nc, S5_FLAT), lambda i, j: (i, j, 0, 0)),
        out_shape=jax.ShapeDtypeStruct((b, G, nc, S5_FLAT), BF16),
        compiler_params=_cparams(("parallel", "parallel")), name="s5",
    )(uf, w1, f_mat, la, lb, d_flat)
    return y.reshape(b, G, nc, S5_L, S5_GROUP).transpose(0, 2, 3, 1, 4).reshape(b, s, S5_WIDTH)


def _split_dot(x, e):
    hi = x.astype(BF16)
    lo = (x - hi.astype(F32)).astype(BF16)
    return jnp.dot(hi, e, preferred_element_type=F32) + jnp.dot(lo, e, preferred_element_type=F32)


def _ssd_kernel(xc_ref, dt_ref, dtb_ref, alog_ref, dexp_ref, e_ref, y_ref, state):
    L, PAD = M2_CHUNK, 8

    @pl.when(pl.program_id(1) == 0)
    def _():
        state[...] = jnp.zeros_like(state)

    xc = xc_ref[0].astype(F32)
    xs = xc[:, :M2_INNER]
    nbc = M2_GROUPS * M2_STATE
    bm = xc[:, M2_INNER:M2_INNER + nbc]
    cm = xc[:, M2_INNER + nbc:]

    dt = jax.nn.softplus(dt_ref[0] + dtb_ref[...])
    da = dt * (-jnp.exp(alog_ref[...]))
    ti = lax.broadcasted_iota(jnp.int32, (L, L), 0)
    si = lax.broadcasted_iota(jnp.int32, (L, L), 1)
    causal = si <= ti
    tril =jnp.where(causal, 1.0, 0.0).astype(BF16)
    hi = da.astype(BF16)
    r1 = da - hi.astype(F32)
    mid = r1.astype(BF16)
    lo = (r1 - mid.astype(F32)).astype(BF16)
    cs = (jnp.dot(tril, hi, preferred_element_type=F32) + jnp.dot(tril, mid, preferred_element_type=F32)
          + jnp.dot(tril, lo, preferred_element_type=F32))
    cs2 = cs * math.log2(math.e)
    cs2_t = cs2.T
    cs_end = cs[L - 1:L, :]
    stack = jnp.concatenate([dt, dt * jnp.exp(cs_end - cs), jnp.exp(cs)], axis=0).astype(BF16)
    fac = jnp.dot(stack, e_ref[...], preferred_element_type=F32)
    dt_e, dw_e, ecs_e = fac[:L], fac[L:2 * L], fac[2 * L:]
    dend_e = _split_dot(jnp.broadcast_to(jnp.exp(cs_end), (PAD, LANES)), e_ref[...])[0:1]
    x_dt = (xs * dt_e).astype(BF16)
    x_w = (xs * dw_e).astype(BF16)

    lane = lax.broadcasted_iota(jnp.int32, (L, LANES), 1)
    zero = jnp.zeros((L, LANES), BF16)
    for g in range(M2_GROUPS):
        gs = slice(g * M2_GW, (g + 1) * M2_GW)
        b_g = bm[:, g * M2_STATE:(g + 1) * M2_STATE]
        c_g = cm[:, g * M2_STATE:(g + 1) * M2_STATE].astype(BF16)
        cb = lax.dot_general(c_g, b_g.astype(BF16), (((1,), (1,)), ((), ())), preferred_element_type=F32)
        s_old = state[g]
        y_off = jnp.dot(c_g, s_old.astype(BF16), preferred_element_type=F32) * ecs_e[:, gs]
        state[g] = s_old * dend_e[:, gs] + jnp.dot(b_g.T.astype(BF16), x_w[:, gs], preferred_element_type=F32)
        for pr in range(M2_GW // LANES):
            ps = slice(g * M2_GW + pr * LANES, g * M2_GW + (pr + 1) * LANES)
            xp = x_dt[:, ps]
            y_pair = None
            for hh in range(2):
                h = (g * M2_GW + pr * LANES) // M2_HEADDIM + hh
                seg = jnp.exp2(jnp.where(causal, cs2[:, h:h + 1] - cs2_t[h:h + 1, :], -jnp.inf))
                att = (cb * seg).astype(BF16)
                x_h = jnp.where((lane < M2_HEADDIM) == (hh == 0), xp, zero)
                part = jnp.dot(att, x_h, preferred_element_type=F32)
                y_pair = part if y_pair is None else y_pair + part
            y = y_pair + y_off[:, pr * LANES:(pr + 1) * LANES] + dexp_ref[:, ps] * xs[:, ps]
            y_ref[0, :, ps] = y.astype(y_ref.dtype)


def _ssd(xc, dt_raw, dt_bias, a_log, d):
    b, s, _ = xc.shape
    pad_row = lambda v: jnp.pad(v, (0, LANES - M2_HEADS)).reshape(1, LANES)
    expand = (jnp.arange(LANES)[:, None] == (jnp.arange(M2_INNER) // M2_HEADDIM)[None, :]).astype(BF16)
    tok = lambda w: pl.BlockSpec((1, M2_CHUNK, w), lambda i, j: (i, j, 0))
    fixed = lambda shape: pl.BlockSpec(shape, lambda i, j: (0, 0))
    return pl.pallas_call(
        _ssd_kernel, grid=(b, s // M2_CHUNK),
        in_specs=[tok(M2_CONV_DIM), tok(LANES), fixed((1, LANES)), fixed((1, LANES)), fixed((1, M2_INNER)),
                  fixed((LANES, M2_INNER))],
        out_specs=tok(M2_INNER), out_shape=jax.ShapeDtypeStruct((b, s, M2_INNER), BF16),
        scratch_shapes=[pltpu.VMEM((M2_GROUPS, M2_STATE, M2_GW), F32)],
        compiler_params=_cparams(("parallel", "arbitrary")), name="ssd",
    )(xc, dt_raw, pad_row(dt_bias), pad_row(a_log), jnp.repeat(d, M2_HEADDIM).reshape(1, M2_INNER), expand)


def _cols(w, sizes):
    idx, out = 0, []
    for n in sizes:
        out.append(w[:, idx:idx + n])
        idx += n
    return out


def _even_layer(h, b, s, i, layer, p, mem_k, mem_v, rope_t, tm, tq):
    w_u, w_za, w_cq, w_ckv, w_kr, w_zb, w_qm, w_zm = _cols(
        p["ev_w_in"][i], (S5_WIDTH, S5_WIDTH, MLA_Q_LORA, MLA_KV_LORA, MLA_ROPE, MLA_WIDTH, MEM_WIDTH, MEM_WIDTH))
    w_kr = jnp.pad(w_kr, ((0, 0), (MLA_NOPE, LANES - MLA_QK)))
    weights = [w.astype(BF16) for w in (w_u, w_za, w_zb, w_qm, w_zm, w_cq, w_ckv, w_kr)]
    u, za, zb, qm, zm, cq, ckv, kr = _rms_proj(h, p["norm_g"][layer], weights, [BF16] * 5 + [F32] * 3, tm)

    n_steps = max(1, (s // S5_L - 1).bit_length())
    tables = _s5_tables(p["s5_a_re"][i], p["s5_a_im"][i], p["s5_log_dt"][i], p["s5_b_re"][i], p["s5_b_im"][i],
                        p["s5_c_re"][i], p["s5_c_im"][i], n_steps)
    y_s5 = _s5(u.reshape(b, s, S5_WIDTH), tables, p["s5_d"][i], gb=8).reshape(b * s, S5_WIDTH)

    qt, k, vt = _mla_prep(cq, ckv, kr, rope_t, p["mla_q_a_norm_g"][i], p["mla_w_uq"][i], p["mla_kv_a_norm_g"][i],
                          p["mla_w_ukv"][i], p["mla_q_norm_g"][i], p["mla_k_norm_g"][i], b, s, tq)
    y_b = _mla_attn(qt, k.reshape(b, s, MLA_HEADS * LANES), vt, zb.reshape(b, s, MLA_WIDTH),
                    tq).reshape(b * s, MLA_WIDTH)

    y_m = _mem_attn(qm.reshape(b, s, MEM_WIDTH), zm.reshape(b, s, MEM_WIDTH), mem_k, mem_v,
                    p["mem_q_norm_g"][layer], layer, 2 * tm if s % (2 * tm) == 0 else tm).reshape(b * s, MEM_WIDTH)

    w_out = p["ev_w_out"][i].astype(BF16)
    ws = [w_out[:S5_WIDTH], w_out[S5_WIDTH:S5_WIDTH + MLA_WIDTH], w_out[S5_WIDTH + MLA_WIDTH:]]
    glu = (za, p["s5_glu_w"][i].astype(BF16), p["s5_glu_b"][i].reshape(1, S5_WIDTH))
    return _out_proj(h, [y_s5, y_b, y_m], ws, tm, glu)


def _odd_layer(h, b, s, i, layer, p, mem_k, mem_v, tm):
    w_z, w_xbc, w_dt, w_qm, w_zm = _cols(p["od_w_in"][i], (M2_INNER, M2_CONV_DIM, M2_HEADS, MEM_WIDTH, MEM_WIDTH))
    w_dt = jnp.pad(w_dt, ((0, 0), (0, LANES - M2_HEADS)))
    weights = [w.astype(BF16) for w in (w_z, w_xbc, w_qm, w_zm, w_dt)]
    z, xc, qm, zm, dt_raw = _rms_proj(h, p["norm_g"][layer], weights, [BF16] * 4 + [F32], tm,
                                      conv=(1, p["m2_conv_w"][i], p["m2_conv_b"][i]), seq_len=s)
    y_c = _ssd(xc.reshape(b, s, M2_CONV_DIM), dt_raw.reshape(b, s, LANES), p["m2_dt_bias"][i], p["m2_a_log"][i],
               p["m2_d"][i]).reshape(b * s, M2_INNER)
    y_m = _mem_attn(qm.reshape(b, s, MEM_WIDTH), zm.reshape(b, s, MEM_WIDTH), mem_k, mem_v,
                    p["mem_q_norm_g"][layer], layer, 2 * tm if s % (2 * tm) == 0 else tm).reshape(b * s, MEM_WIDTH)
    w_out = p["od_w_out"][i].astype(BF16)
    return _out_proj(h, [y_c, y_m], [w_out[:M2_INNER], w_out[M2_INNER:]], tm,
                     m2_args=(z, p["m2_norm_g"][i].reshape(1, M2_INNER)))


def _token_tile(s):
    return 512 if s % 512 == 0 else s


def kernel(x, mem, positions, norm_g, mem_norm_g, mem_w_kv, mem_q_norm_g, mem_k_norm_g, ev_w_in, ev_w_out, s5_a_re, s5_a_im, s5_log_dt, s5_b_re, s5_b_im, s5_c_re, s5_c_im, s5_d, s5_glu_w, s5_glu_b, mla_q_a_norm_g, mla_w_uq, mla_kv_a_norm_g, mla_w_ukv, mla_q_norm_g, mla_k_norm_g, od_w_in, od_w_out, m2_conv_w, m2_conv_b, m2_dt_bias, m2_a_log, m2_d, m2_norm_g):
    p = dict(norm_g=norm_g, mem_q_norm_g=mem_q_norm_g, ev_w_in=ev_w_in, ev_w_out=ev_w_out,
             s5_a_re=s5_a_re, s5_a_im=s5_a_im, s5_log_dt=s5_log_dt, s5_b_re=s5_b_re, s5_b_im=s5_b_im,
             s5_c_re=s5_c_re, s5_c_im=s5_c_im, s5_d=s5_d, s5_glu_w=s5_glu_w, s5_glu_b=s5_glu_b,
             mla_q_a_norm_g=mla_q_a_norm_g, mla_w_uq=mla_w_uq, mla_kv_a_norm_g=mla_kv_a_norm_g,
             mla_w_ukv=mla_w_ukv, mla_q_norm_g=mla_q_norm_g, mla_k_norm_g=mla_k_norm_g,
             od_w_in=od_w_in, od_w_out=od_w_out, m2_conv_w=m2_conv_w, m2_conv_b=m2_conv_b,
             m2_dt_bias=m2_dt_bias, m2_a_log=m2_a_log, m2_d=m2_d, m2_norm_g=m2_norm_g)
    b, s, d = x.shape
    tm = _token_tile(s)
    tq = 1024 if s % 1024 == 0 else s
    mem_k, mem_v = _mem_kv(mem, mem_norm_g, mem_w_kv, mem_k_norm_g)
    rope_t = _rope_tables(positions)
    h = x.reshape(b * s, d)
    for layer in range(DEPTH):
        if layer % 2 == 0:
            h = _even_layer(h, b, s, layer // 2, layer, p, mem_k, mem_v, rope_t, tm, tq)
        else:
            h = _odd_layer(h, b, s, layer // 2, layer, p, mem_k, mem_v, tm)
    return h.reshape(b, s, d)
```

```python
import functools
import math

import jax
import jax.numpy as jnp
from jax import lax
from jax.experimental import pallas as pl
from jax.experimental.pallas import tpu as pltpu

F32 = jnp.float32
BF16 = jnp.bfloat16

D_MODEL = 1024
DEPTH = 4
CHUNK = 64
N_MEM = 256
RMS_EPS = 1e-6

S5_WIDTH = 512
S5_GROUP = 16
S5_GROUPS = S5_WIDTH // S5_GROUP
S5_STATE = 64
S5_L = 16
S5_FLAT = S5_L * S5_GROUP

MLA_HEADS = 8
MLA_NOPE = 64
MLA_ROPE = 32
MLA_QK = MLA_NOPE + MLA_ROPE
MLA_V = 64
MLA_WIDTH = MLA_HEADS * MLA_V
MLA_HPS = 2
MLA_VA = MLA_V + 16
MLA_Q_LORA = 256
MLA_KV_LORA = 128
ROPE_BASE = 10000.0
ROPE_HALF = MLA_ROPE // 2
LANES = 128

M2_INNER = 2 * D_MODEL
M2_HEADDIM = 64
M2_HEADS = M2_INNER // M2_HEADDIM
M2_GROUPS = 4
M2_STATE = 128
M2_CONV = 4
M2_CHUNK = 128
M2_CONV_DIM = M2_INNER + 2 * M2_GROUPS * M2_STATE
M2_GW = M2_INNER // M2_GROUPS

MEM_HEADS = 4
MEM_HD = 128
MEM_WIDTH = MEM_HEADS * MEM_HD

VMEM_LIMIT = 56 * 1024 * 1024


def _cparams(sem):
    return pltpu.CompilerParams(dimension_semantics=sem, vmem_limit_bytes=VMEM_LIMIT)


def _silu(z):
    h = 0.5 * z
    return h + h * jnp.tanh(h)


CONV_PAD = 8


def _rms_proj_kernel(h_ref, g_ref, *refs, n_out, col_chunk, conv_idx, tiles_per_seq):
    w_refs = refs[:n_out]
    if conv_idx is None:
        o_refs = refs[n_out:]
    else:
        cw_ref, cb_ref = refs[n_out:n_out + 2]
        o_refs = refs[n_out + 2:2 * n_out + 2]
        pad, tail = refs[2 * n_out + 2:]

        @pl.when(pl.program_id(0) % tiles_per_seq == 0)
        def _():
            tail[...] = jnp.zeros_like(tail)

    x = h_ref[...]
    tm = x.shape[0]
    xn = (x * lax.rsqrt(jnp.mean(x * x, axis=-1, keepdims=True) + RMS_EPS) * g_ref[...]).astype(BF16)
    for i, (w_ref, o_ref) in enumerate(zip(w_refs, o_refs)):
        n = w_ref.shape[1]
        for c0 in range(0, n, col_chunk):
            c1 = min(n, c0 + col_chunk)
            r = jnp.dot(xn, w_ref[:, c0:c1], preferred_element_type=F32)
            if i == conv_idx:
                pad[0:CONV_PAD, :] = tail[:, c0:c1]
                pad[CONV_PAD:CONV_PAD + tm, :] = r
                first = CONV_PAD - (M2_CONV - 1)
                acc = cb_ref[:, c0:c1] + cw_ref[0:1, c0:c1] * pad[first:first + tm, :]
                for k in range(1, M2_CONV):
                    acc = acc + cw_ref[k:k + 1, c0:c1] * pad[first + k:first + k + tm, :]
                tail[:, c0:c1] = pad[tm:tm + CONV_PAD, :]
                r = _silu(acc)
            o_ref[:, c0:c1] = r.astype(o_ref.dtype)


def _rms_proj(h, g, weights, out_dtypes, tm, conv=None, seq_len=None):
    t, d = h.shape
    col_chunk = 256 if conv is not None else 512
    fixed = lambda i: (0, 0)
    in_specs = [pl.BlockSpec((tm, d), lambda i: (i, 0)), pl.BlockSpec((1, d), fixed)]
    in_specs += [pl.BlockSpec(w.shape, fixed) for w in weights]
    args = [h, g.reshape(1, d), *weights]
    scratch, conv_idx, tiles_per_seq = [], None, None
    if conv is not None:
        conv_idx, cw, cb = conv
        n = weights[conv_idx].shape[1]
        in_specs += [pl.BlockSpec(cw.shape, fixed), pl.BlockSpec((1, n), fixed)]
        args += [cw, cb.reshape(1, n)]
        scratch = [pltpu.VMEM((tm + CONV_PAD, col_chunk), F32), pltpu.VMEM((CONV_PAD, n), F32)]
        tiles_per_seq = seq_len // tm
    out_specs = [pl.BlockSpec((tm, w.shape[1]), lambda i: (i, 0)) for w in weights]
    out_shape = [jax.ShapeDtypeStruct((t, w.shape[1]), dt) for w, dt in zip(weights, out_dtypes)]
    return pl.pallas_call(
        functools.partial(_rms_proj_kernel, n_out=len(weights), col_chunk=col_chunk, conv_idx=conv_idx,
                          tiles_per_seq=tiles_per_seq),
        grid=(t // tm,), in_specs=in_specs, out_specs=out_specs, out_shape=out_shape, scratch_shapes=scratch,
        compiler_params=_cparams(("arbitrary",)), name="rms_proj",
    )(*args)


def _out_proj_kernel(h_ref, *refs, n_in, pre):
    x_refs, w_refs = refs[:n_in], refs[n_in:2 * n_in]
    extra = refs[2 * n_in:-1]
    o_ref = refs[-1]
    acc = h_ref[...]
    for i, (x_ref, w_ref) in enumerate(zip(x_refs, w_refs)):
        if i == 0 and pre == "m2":
            z_ref, ng_ref = extra
            for g in range(M2_GROUPS):
                gs = slice(g * M2_GW, (g + 1) * M2_GW)
                gated = x_ref[:, gs].astype(F32) * _silu(z_ref[:, gs].astype(F32))
                xn = gated * lax.rsqrt(jnp.mean(gated * gated, axis=-1, keepdims=True) + RMS_EPS) * ng_ref[:, gs]
                acc = acc + jnp.dot(xn.astype(BF16), w_ref[gs, :], preferred_element_type=F32)
            continue
        x = x_ref[...]
        if i == 0 and pre == "glu":
            za_ref, gw_ref, gb_ref = extra
            gate = jnp.dot(x, gw_ref[...], preferred_element_type=F32) + gb_ref[...]
            x = (x.astype(F32) * jax.nn.sigmoid(gate) * _silu(za_ref[...].astype(F32))).astype(BF16)
        acc = acc + jnp.dot(x, w_ref[...], preferred_element_type=F32)
    o_ref[...] = acc


def _out_proj(h, xs, ws, tm, glu_args=None, m2_args=None):
    t, d = h.shape
    row = lambda i: (i, 0)
    fixed = lambda i: (0, 0)
    in_specs = [pl.BlockSpec((tm, d), row)]
    in_specs += [pl.BlockSpec((tm, x.shape[1]), row) for x in xs]
    in_specs += [pl.BlockSpec(w.shape, fixed) for w in ws]
    args = [h, *xs, *ws]
    if glu_args is not None:
        za, gw, gb = glu_args
        in_specs += [pl.BlockSpec((tm, za.shape[1]), row), pl.BlockSpec(gw.shape, fixed),
                     pl.BlockSpec(gb.shape, fixed)]
        args += [za, gw, gb]
    if m2_args is not None:
        z, ng = m2_args
        in_specs += [pl.BlockSpec((tm, z.shape[1]), row), pl.BlockSpec(ng.shape, fixed)]
        args += [z, ng]
    pre = "glu" if glu_args is not None else "m2" if m2_args is not None else None
    return pl.pallas_call(
        functools.partial(_out_proj_kernel, n_in=len(xs), pre=pre),
        grid=(t // tm,), in_specs=in_specs, out_specs=pl.BlockSpec((tm, d), row),
        out_shape=jax.ShapeDtypeStruct((t, d), F32),
        compiler_params=_cparams(("parallel",)), name="out_proj",
    )(*args)


def _mem_kv_kernel(mem_ref, g_ref, w_ref, kg_ref, k_ref, v_ref):
    x = mem_ref[0]
    xn = (x * lax.rsqrt(jnp.mean(x * x, axis=-1, keepdims=True) + RMS_EPS) * g_ref[0]).astype(BF16)
    kv = jnp.dot(xn, w_ref[0], preferred_element_type=F32)
    for h in range(MEM_HEADS):
        kh = kv[:, h * MEM_HD:(h + 1) * MEM_HD]
        kn = kh * lax.rsqrt(jnp.mean(kh * kh, axis=-1, keepdims=True) + RMS_EPS) * kg_ref[0]
        k_ref[0, 0, :, h * MEM_HD:(h + 1) * MEM_HD] = kn.astype(BF16)
    v_ref[0, 0] = kv[:, MEM_WIDTH:].astype(BF16)


def _mem_kv(mem, mem_norm_g, w_kv, k_norm_g):
    b = mem.shape[0]
    out = jax.ShapeDtypeStruct((DEPTH, b, N_MEM, MEM_WIDTH), BF16)
    return pl.pallas_call(
        _mem_kv_kernel, grid=(DEPTH, b),
        in_specs=[pl.BlockSpec((1, N_MEM, D_MODEL), lambda l, i: (i, 0, 0)),
                  pl.BlockSpec((1, 1, D_MODEL), lambda l, i: (l, 0, 0)),
                  pl.BlockSpec((1, D_MODEL, 2 * MEM_WIDTH), lambda l, i: (l, 0, 0)),
                  pl.BlockSpec((1, 1, MEM_HD), lambda l, i: (l, 0, 0))],
        out_specs=[pl.BlockSpec((1, 1, N_MEM, MEM_WIDTH), lambda l, i: (l, i, 0, 0))] * 2,
        out_shape=[out, out], compiler_params=_cparams(("arbitrary", "arbitrary")), name="mem_kv",
    )(mem, mem_norm_g.reshape(DEPTH, 1, D_MODEL), w_kv.astype(BF16), k_norm_g.reshape(DEPTH, 1, MEM_HD))


def _mem_attn_kernel(q_ref, z_ref, k_ref, v_ref, qg_ref, o_ref):
    scale = 1.0 / math.sqrt(MEM_HD)
    for h in range(MEM_HEADS):
        sl = slice(h * MEM_HD, (h + 1) * MEM_HD)
        q = q_ref[0, :, sl].astype(F32)
        qn = (q * lax.rsqrt(jnp.mean(q * q, axis=-1, keepdims=True) + RMS_EPS) * (qg_ref[...] * scale)).astype(BF16)
        s = lax.dot_general(qn, k_ref[0, 0, :, sl], (((1,), (1,)), ((), ())), preferred_element_type=F32)
        p = jnp.exp(s - jnp.max(s, axis=-1, keepdims=True))
        l = jnp.sum(p, axis=-1, keepdims=True)
        o = jnp.dot(p.astype(BF16), v_ref[0, 0, :, sl], preferred_element_type=F32) / l
        o_ref[0, :, sl] = (o * _silu(z_ref[0, :, sl].astype(F32))).astype(BF16)


def _mem_attn(q, z, k_all, v_all, qg, layer, tq):
    b, s, _ = q.shape
    tok = pl.BlockSpec((1, tq, MEM_WIDTH), lambda i, j: (i, j, 0))
    bank = pl.BlockSpec((1, 1, N_MEM, MEM_WIDTH), lambda i, j: (layer, i, 0, 0))
    return pl.pallas_call(
        _mem_attn_kernel, grid=(b, s // tq),
        in_specs=[tok, tok, bank, bank, pl.BlockSpec((1, MEM_HD), lambda i, j: (0, 0))],
        out_specs=tok, out_shape=jax.ShapeDtypeStruct((b, s, MEM_WIDTH), BF16),
        compiler_params=_cparams(("parallel", "parallel")), name="mem_attn",
    )(q, z, k_all, v_all, qg.reshape(1, MEM_HD))


def _rope_table_kernel(pos_ref, inv_ref, cos_ref, sin_ref):
    ang = pos_ref[0].astype(F32) * inv_ref[...]
    cos_ref[0] = jnp.cos(ang)
    sin_ref[0] = jnp.sin(ang)


def _rope_tables(positions):
    b, s = positions.shape
    inv = ROPE_BASE ** (-jnp.arange(ROPE_HALF, dtype=F32) / ROPE_HALF)
    out = jax.ShapeDtypeStruct((b, ROPE_HALF, s), F32)
    cos, sin = pl.pallas_call(
        _rope_table_kernel, grid=(b,),
        in_specs=[pl.BlockSpec((1, 1, s), lambda i: (i, 0, 0)), pl.BlockSpec((ROPE_HALF, 1), lambda i: (0, 0))],
        out_specs=[pl.BlockSpec((1, ROPE_HALF, s), lambda i: (i, 0, 0))] * 2,
        out_shape=[out, out], compiler_params=_cparams(("parallel",)), name="rope_tables",
    )(positions.reshape(b, 1, s), inv.reshape(ROPE_HALF, 1))
    tail = LANES - MLA_QK
    cos_q = jnp.concatenate([cos, cos], 1)
    sin_q = jnp.concatenate([-sin, sin], 1)
    cos_k = jnp.concatenate([jnp.ones((b, s, MLA_NOPE), F32), jnp.swapaxes(cos_q, 1, 2), jnp.ones((b, s, tail), F32)], -1)
    sin_k = jnp.concatenate([jnp.zeros((b, s, MLA_NOPE), F32), jnp.swapaxes(sin_q, 1, 2), jnp.zeros((b, s, tail), F32)], -1)
    return cos_q, sin_q, cos_k.reshape(b * s, LANES), sin_k.reshape(b * s, LANES)


def _head_norm_rope(x, gain, cos, sin, lane):
    xn = x * lax.rsqrt(jnp.sum(x * x, axis=-1, keepdims=True) * (1.0 / MLA_QK) + RMS_EPS) * gain
    partner = jnp.where(lane < MLA_NOPE + ROPE_HALF,
                        pltpu.roll(xn, LANES - ROPE_HALF, 1), pltpu.roll(xn, ROPE_HALF, 1))
    return xn * cos + partner * sin


def _mla_prep_kernel(cq_ref, ckv_ref, kr_ref, cosq_ref, sinq_ref, cosk_ref, sink_ref, gqa_ref, wqt_ref, gkva_ref,
                     wk_ref, wvt_ref, gq_ref, gk_ref, qt_ref, k_ref, vt_ref):
    cq = cq_ref[...]
    cqn = cq * lax.rsqrt(jnp.mean(cq * cq, axis=-1, keepdims=True) + RMS_EPS) * gqa_ref[...]
    ckv = ckv_ref[...]
    ckvn = ckv * lax.rsqrt(jnp.mean(ckv * ckv, axis=-1, keepdims=True) + RMS_EPS) * gkva_ref[...]
    cqn_t = cqn.T.astype(BF16)
    ckvn_t = ckvn.T.astype(BF16)
    ckvn = ckvn.astype(BF16)
    kr = kr_ref[...]
    cosq, sinq, cosk, sink = cosq_ref[0], sinq_ref[0], cosk_ref[...], sink_ref[...]
    lane = lax.broadcasted_iota(jnp.int32, cosk.shape, 1)
    qscale = math.log2(math.e) / math.sqrt(MLA_QK)
    tm = cq.shape[0]
    ones_tile = jnp.where(lax.broadcasted_iota(jnp.int32, (MLA_VA - MLA_V, tm), 0) == 0, 1.0, 0.0).astype(BF16)
    for h in range(MLA_HEADS):
        sl = slice(h * LANES, (h + 1) * LANES)
        qt = jnp.dot(wqt_ref[sl, :], cqn_t, preferred_element_type=F32)
        qn = qt * lax.rsqrt(jnp.sum(qt * qt, axis=0, keepdims=True) * (1.0 / MLA_QK) + RMS_EPS) * gq_ref[...]
        lo, hi = h * LANES + MLA_NOPE, h * LANES + MLA_QK
        partner = jnp.concatenate([qn[MLA_NOPE + ROPE_HALF:MLA_QK], qn[MLA_NOPE:MLA_NOPE + ROPE_HALF]], axis=0)
        qt_ref[0, 0, h * LANES:lo, :] = (qn[:MLA_NOPE] * qscale).astype(BF16)
        qt_ref[0, 0, lo:hi, :] = ((qn[MLA_NOPE:MLA_QK] * cosq + partner * sinq) * qscale).astype(BF16)
        qt_ref[0, 0, hi:(h + 1) * LANES, :] = jnp.zeros((LANES - MLA_QK, tm), BF16)
        kh = jnp.dot(ckvn, wk_ref[:, sl], preferred_element_type=F32) + kr
        k_ref[:, sl] = _head_norm_rope(kh, gk_ref[...], cosk, sink, lane).astype(BF16)
        vs = slice(h * MLA_V, (h + 1) * MLA_V)
        vt_ref[0, 0, h * MLA_VA:h * MLA_VA + MLA_V, :] = jnp.dot(wvt_ref[vs, :], ckvn_t,
                                                                 preferred_element_type=F32).astype(BF16)
        vt_ref[0, 0, h * MLA_VA + MLA_V:(h + 1) * MLA_VA, :] = ones_tile


def _pad_heads(w, n_heads, width, offset=0):
    k = w.shape[0]
    w = w.reshape(k, n_heads, width)
    w = jnp.pad(w, ((0, 0), (0, 0), (offset, LANES - width - offset)))
    return w.reshape(k, n_heads * LANES)


def _mla_prep(cq, ckv, kr, rope_t, gqa, w_uq, gkva, w_ukv, gq, gk, b, s, tm):
    cos_q, sin_q, cos_k, sin_k = rope_t
    nt = s // tm
    wqt = _pad_heads(w_uq, MLA_HEADS, MLA_QK).T.astype(BF16)
    w_ukv = w_ukv.reshape(MLA_KV_LORA, MLA_HEADS, MLA_NOPE + MLA_V)
    wk = _pad_heads(w_ukv[:, :, :MLA_NOPE].reshape(MLA_KV_LORA, -1), MLA_HEADS, MLA_NOPE).astype(BF16)
    wvt = w_ukv[:, :, MLA_NOPE:].reshape(MLA_KV_LORA, MLA_WIDTH).T.astype(BF16)
    pad_gain = lambda g: jnp.pad(g, (0, LANES - MLA_QK))
    row = lambda i, j: (i * nt + j, 0)
    fixed = lambda i, j: (0, 0)
    hw = MLA_HEADS * LANES
    return pl.pallas_call(
        _mla_prep_kernel, grid=(b, nt),
        in_specs=[pl.BlockSpec((tm, MLA_Q_LORA), row), pl.BlockSpec((tm, MLA_KV_LORA), row),
                  pl.BlockSpec((tm, LANES), row),
                  pl.BlockSpec((1, MLA_ROPE, tm), lambda i, j: (i, 0, j)),
                  pl.BlockSpec((1, MLA_ROPE, tm), lambda i, j: (i, 0, j)),
                  pl.BlockSpec((tm, LANES), row), pl.BlockSpec((tm, LANES), row),
                  pl.BlockSpec((1, MLA_Q_LORA), fixed), pl.BlockSpec((hw, MLA_Q_LORA), fixed),
                  pl.BlockSpec((1, MLA_KV_LORA), fixed), pl.BlockSpec((MLA_KV_LORA, hw), fixed),
                  pl.BlockSpec((MLA_WIDTH, MLA_KV_LORA), fixed),
                  pl.BlockSpec((LANES, 1), fixed), pl.BlockSpec((1, LANES), fixed)],
        out_specs=[pl.BlockSpec((1, 1, hw, tm), lambda i, j: (i, j, 0, 0)), pl.BlockSpec((tm, hw), row),
                   pl.BlockSpec((1, 1, MLA_HEADS * MLA_VA, tm), lambda i, j: (i, j, 0, 0))],
        out_shape=[jax.ShapeDtypeStruct((b, nt, hw, tm), BF16), jax.ShapeDtypeStruct((b * s, hw), BF16),
                   jax.ShapeDtypeStruct((b, nt, MLA_HEADS * MLA_VA, tm), BF16)],
        compiler_params=_cparams(("parallel", "parallel")), name="mla_prep",
    )(cq, ckv, kr, cos_q, sin_q, cos_k, sin_k, gqa.reshape(1, -1), wqt, gkva.reshape(1, -1), wk, wvt,
      pad_gain(gq).reshape(LANES, 1), pad_gain(gk).reshape(1, LANES))


def _mla_attn_kernel(q_ref, k_ref, v_ref, z_ref, o_ref, m_sc, acc_sc, *, tq):
    qi = pl.program_id(2)
    m_sc[...] = jnp.full(m_sc.shape, -jnp.inf, F32)
    acc_sc[...] = jnp.zeros(acc_sc.shape, F32)

    def block(ki, masked):
        start = pl.multiple_of(ki * tq, tq)
        for h in range(MLA_HPS):
            s = jnp.dot(k_ref[0, pl.ds(start, tq), h * LANES:(h + 1) * LANES], q_ref[0, 0, h * LANES:(h + 1) * LANES, :],
                        preferred_element_type=F32)
            if masked:
                kc = lax.broadcasted_iota(jnp.int32, s.shape, 0) // CHUNK
                qc = lax.broadcasted_iota(jnp.int32, s.shape, 1) // CHUNK
                s = jnp.where(kc <= qc, s, -jnp.inf)
            m_prev = m_sc[h]
            m_new = jnp.maximum(m_prev, jnp.max(s, axis=0, keepdims=True))
            alpha = jnp.exp2(m_prev - m_new)
            p = jnp.exp2(s - m_new)
            acc_sc[h] = alpha * acc_sc[h] + jnp.dot(v_ref[0, ki, h * MLA_VA:(h + 1) * MLA_VA, :], p.astype(BF16),
                                                    preferred_element_type=F32)
            m_sc[h] = m_new

    def body(ki, carry):
        block(ki, masked=False)
        return carry

    lax.fori_loop(0, qi, body, 0)
    block(qi, masked=True)
    o_t = jnp.concatenate([acc_sc[h, :MLA_V, :] / acc_sc[h, MLA_V:MLA_V + 1, :] for h in range(MLA_HPS)], axis=0)
    o_ref[0] = (o_t.T * _silu(z_ref[0].astype(F32))).astype(BF16)


def _mla_attn(qt, k, vt, z, tq):
    b, s, _ = k.shape
    nt = s // tq
    pairs = MLA_HEADS // MLA_HPS
    return pl.pallas_call(
        functools.partial(_mla_attn_kernel, tq=tq), grid=(b, pairs, nt),
        in_specs=[pl.BlockSpec((1, 1, MLA_HPS * LANES, tq), lambda i, p, j: (i, j, p, 0)),
                  pl.BlockSpec((1, s, MLA_HPS * LANES), lambda i, p, j: (i, 0, p)),
                  pl.BlockSpec((1, nt, MLA_HPS * MLA_VA, tq), lambda i, p, j: (i, 0, p, 0)),
                  pl.BlockSpec((1, tq, MLA_HPS * MLA_V), lambda i, p, j: (i, j, p))],
        out_specs=pl.BlockSpec((1, tq, MLA_HPS * MLA_V), lambda i, p, j: (i, j, p)),
        out_shape=jax.ShapeDtypeStruct((b, s, MLA_WIDTH), BF16),
        scratch_shapes=[pltpu.VMEM((MLA_HPS, 1, tq), F32), pltpu.VMEM((MLA_HPS, MLA_VA, tq), F32)],
        compiler_params=_cparams(("parallel", "parallel", "arbitrary")), name="mla_attn",
    )(qt, k, vt, z)


def _s5_tables(a_re, a_im, log_dt, b_re, b_im, c_re, c_im, n_steps):
    hp = lax.Precision.HIGHEST
    dt = jnp.exp(log_dt)[:, None]
    mag = jnp.exp(a_re * dt)
    ab_re, ab_im = mag * jnp.cos(a_im * dt), mag * jnp.sin(a_im * dt)
    den = a_re * a_re + a_im * a_im
    n_re, n_im = ab_re - 1.0, ab_im
    f_re = (n_re * a_re + n_im * a_im) / den
    f_im = (n_im * a_re - n_re * a_im) / den
    bb_re = f_re[..., None] * b_re - f_im[..., None] * b_im
    bb_im = f_re[..., None] * b_im + f_im[..., None] * b_re

    def lam_pow(k):
        k = k.astype(F32)[:, None, None]
        m = jnp.exp(k * a_re * dt)
        return m * jnp.cos(k * a_im * dt), m * jnp.sin(k * a_im * dt)

    L, C, P, G = S5_L, S5_GROUP, S5_STATE, S5_GROUPS
    lr, li = lam_pow(jnp.arange(L + 1))
    w_re = lr[..., None] * bb_re[None] - li[..., None] * bb_im[None]
    w_im = lr[..., None] * bb_im[None] + li[..., None] * bb_re[None]
    kk = (jnp.einsum("gdp,kgpc->kgdc", c_re, w_re[:L], precision=hp)
          - jnp.einsum("gdp,kgpc->kgdc", c_im, w_im[:L], precision=hp))
    lag = jnp.arange(L)[None, :] - jnp.arange(L)[:, None]
    kfull = jnp.where((lag >= 0)[:, :, None, None, None], kk[jnp.clip(lag, 0, L - 1)], 0.0)
    mt = jnp.transpose(kfull, (2, 0, 4, 1, 3)).reshape(G, L * C, L * C)
    e_re = jnp.transpose(w_re[:L][::-1], (1, 0, 3, 2)).reshape(G, L * C, P)
    e_im = jnp.transpose(w_im[:L][::-1], (1, 0, 3, 2)).reshape(G, L * C, P)
    w1 = jnp.concatenate([mt, e_re, e_im, -e_im, e_re], axis=-1)
    g_re = c_re[None] * lr[1:, :, None, :] - c_im[None] * li[1:, :, None, :]
    g_im = c_re[None] * li[1:, :, None, :] + c_im[None] * lr[1:, :, None, :]
    f_mat = jnp.concatenate([jnp.transpose(g_re, (1, 3, 0, 2)).reshape(G, P, L * C),
                             -jnp.transpose(g_im, (1, 3, 0, 2)).reshape(G, P, L * C)], axis=1)
    sr, si = lam_pow(L * (2 ** jnp.arange(n_steps)))
    la = jnp.transpose(jnp.concatenate([sr, sr], -1), (1, 0, 2))[:, :, None, :]
    lb = jnp.transpose(jnp.concatenate([si, si], -1), (1, 0, 2))[:, :, None, :]
    return w1.astype(BF16), f_mat.astype(BF16), la, lb


def _s5_kernel(u_ref, w1_ref, f_ref, la_ref, lb_ref, d_ref, y_ref, *, gb, n_steps):
    nc = u_ref.shape[1]
    row = lax.broadcasted_iota(jnp.int32, (nc, 2 * S5_STATE), 0)

    def shift(t, d):
        return jnp.where(row >= d, pltpu.roll(t, d, 0), 0.0)

    for g in range(gb):
        u = u_ref[0, :, :, g * S5_GROUP:(g + 1) * S5_GROUP].reshape(nc, S5_FLAT)
        r = jnp.dot(u, w1_ref[g], preferred_element_type=F32)
        y = r[:, :S5_FLAT]
        x = r[:, S5_FLAT:S5_FLAT + 2 * S5_STATE]
        xs = r[:, S5_FLAT + 2 * S5_STATE:]
        for s in range(n_steps):
            a, b = la_ref[g, s], lb_ref[g, s]
            tx, txs = a * x + b * xs, a * xs - b * x
            x, xs = x + shift(tx, 1 << s), xs + shift(txs, 1 << s)
        h_in = shift(x, 1).astype(BF16)
        y = y + jnp.dot(h_in, f_ref[g], preferred_element_type=F32) + d_ref[g] * u.astype(F32)
        y_ref[0, :, :, g * S5_GROUP:(g + 1) * S5_GROUP] = (
            jax.nn.gelu(y, approximate=True).reshape(nc, S5_L, S5_GROUP).astype(BF16))


def _s5(u, tables, d, gb):
    w1, f_mat, la, lb = tables
    b, s, _ = u.shape
    nc = s // S5_L
    n_steps = la.shape[1]
    G = S5_GROUPS
    uf = u.reshape(b, nc, S5_L, S5_WIDTH)
    d_flat = jnp.tile(d.reshape(G, 1, S5_GROUP), (1, 1, S5_L))
    grp = lambda i, j: (j, 0, 0)
    y = pl.pallas_call(
        functools.partial(_s5_kernel, gb=gb, n_steps=n_steps), grid=(b, G // gb),
        in_specs=[pl.BlockSpec((1, nc, S5_L, gb * S5_GROUP), lambda i, j: (i, 0, 0, j)),
                  pl.BlockSpec((gb, S5_FLAT, 2 * S5_FLAT), grp),
                  pl.BlockSpec((gb, 2 * S5_STATE, S5_FLAT), grp),
                  pl.BlockSpec((gb, n_steps, 1, 2 * S5_STATE), lambda i, j: (j, 0, 0, 0)),
                  pl.BlockSpec((gb, n_steps, 1, 2 * S5_STATE), lambda i, j: (j, 0, 0, 0)),
                  pl.BlockSpec((gb, 1, S5_FLAT), grp)],
        out_specs=pl.BlockSpec((1, nc, S5_L, gb * S5_GROUP), lambda i, j: (i, 0, 0, j)),
        out_shape=jax.ShapeDtypeStruct((b, nc, S5_L, S5_WIDTH), BF16),
        compiler_params=_cparams(("parallel", "parallel")), name="s5",
    )(uf, w1, f_mat, la, lb, d_flat)
    return y.reshape(b, s, S5_WIDTH)


def _split_dot(x, e):
    hi = x.astype(BF16)
    lo = (x - hi.astype(F32)).astype(BF16)
    return jnp.dot(hi, e, preferred_element_type=F32) + jnp.dot(lo, e, preferred_element_type=F32)


def _ssd_kernel(xc_ref, dt_ref, dtb_ref, alog_ref, dexp_ref, e_ref, y_ref, state):
    L, PAD = M2_CHUNK, 8

    @pl.when(pl.program_id(1) == 0)
    def _():
        state[...] = jnp.zeros_like(state)

    xc = xc_ref[0].astype(F32)
    xs = xc[:, :M2_INNER]
    nbc = M2_GROUPS * M2_STATE
    bm = xc[:, M2_INNER:M2_INNER + nbc]
    cm = xc[:, M2_INNER + nbc:]

    dt = jax.nn.softplus(dt_ref[0] + dtb_ref[...])
    da = dt * (-jnp.exp(alog_ref[...]))
    ti = lax.broadcasted_iota(jnp.int32, (L, L), 0)
    si = lax.broadcasted_iota(jnp.int32, (L, L), 1)
    causal = si <= ti
    tril =jnp.where(causal, 1.0, 0.0).astype(BF16)
    hi = da.astype(BF16)
    r1 = da - hi.astype(F32)
    mid = r1.astype(BF16)
    lo = (r1 - mid.astype(F32)).astype(BF16)
    cs = (jnp.dot(tril, hi, preferred_element_type=F32) + jnp.dot(tril, mid, preferred_element_type=F32)
          + jnp.dot(tril, lo, preferred_element_type=F32))
    cs2 = cs * math.log2(math.e)
    cs2_t = cs2.T
    cs_end = cs[L - 1:L, :]
    stack = jnp.concatenate([dt, dt * jnp.exp(cs_end - cs), jnp.exp(cs)], axis=0).astype(BF16)
    fac = jnp.dot(stack, e_ref[...], preferred_element_type=F32)
    dt_e, dw_e, ecs_e = fac[:L], fac[L:2 * L], fac[2 * L:]
    dend_e = _split_dot(jnp.broadcast_to(jnp.exp(cs_end), (PAD, LANES)), e_ref[...])[0:1]
    x_dt = (xs * dt_e).astype(BF16)
    x_w = (xs * dw_e).astype(BF16)

    lane = lax.broadcasted_iota(jnp.int32, (L, LANES), 1)
    zero = jnp.zeros((L, LANES), BF16)
    for g in range(M2_GROUPS):
        gs = slice(g * M2_GW, (g + 1) * M2_GW)
        b_g = bm[:, g * M2_STATE:(g + 1) * M2_STATE]
        c_g = cm[:, g * M2_STATE:(g + 1) * M2_STATE].astype(BF16)
        cb = lax.dot_general(c_g, b_g.astype(BF16), (((1,), (1,)), ((), ())), preferred_element_type=F32)
        s_old = state[g]
        y_off = jnp.dot(c_g, s_old.astype(BF16), preferred_element_type=F32) * ecs_e[:, gs]
        state[g] = s_old * dend_e[:, gs] + jnp.dot(b_g.T.astype(BF16), x_w[:, gs], preferred_element_type=F32)
        for pr in range(M2_GW // LANES):
            ps = slice(g * M2_GW + pr * LANES, g * M2_GW + (pr + 1) * LANES)
            xp = x_dt[:, ps]
            y_pair = None
            for hh in range(2):
                h = (g * M2_GW + pr * LANES) // M2_HEADDIM + hh
                seg = jnp.exp2(jnp.where(causal, cs2[:, h:h + 1] - cs2_t[h:h + 1, :], -jnp.inf))
                att = (cb * seg).astype(BF16)
                x_h = jnp.where((lane < M2_HEADDIM) == (hh == 0), xp, zero)
                part = jnp.dot(att, x_h, preferred_element_type=F32)
                y_pair = part if y_pair is None else y_pair + part
            y = y_pair + y_off[:, pr * LANES:(pr + 1) * LANES] + dexp_ref[:, ps] * xs[:, ps]
            y_ref[0, :, ps] = y.astype(y_ref.dtype)


def _ssd(xc, dt_raw, dt_bias, a_log, d):
    b, s, _ = xc.shape
    pad_row = lambda v: jnp.pad(v, (0, LANES - M2_HEADS)).reshape(1, LANES)
    expand = (jnp.arange(LANES)[:, None] == (jnp.arange(M2_INNER) // M2_HEADDIM)[None, :]).astype(BF16)
    tok = lambda w: pl.BlockSpec((1, M2_CHUNK, w), lambda i, j: (i, j, 0))
    fixed = lambda shape: pl.BlockSpec(shape, lambda i, j: (0, 0))
    return pl.pallas_call(
        _ssd_kernel, grid=(b, s // M2_CHUNK),
        in_specs=[tok(M2_CONV_DIM), tok(LANES), fixed((1, LANES)), fixed((1, LANES)), fixed((1, M2_INNER)),
                  fixed((LANES, M2_INNER))],
        out_specs=tok(M2_INNER), out_shape=jax.ShapeDtypeStruct((b, s, M2_INNER), BF16),
        scratch_shapes=[pltpu.VMEM((M2_GROUPS, M2_STATE, M2_GW), F32)],
        compiler_params=_cparams(("parallel", "arbitrary")), name="ssd",
    )(xc, dt_raw, pad_row(dt_bias), pad_row(a_log), jnp.repeat(d, M2_HEADDIM).reshape(1, M2_INNER), expand)


def _cols(w, sizes):
    idx, out = 0, []
    for n in sizes:
        out.append(w[:, idx:idx + n])
        idx += n
    return out


def _even_layer(h, b, s, i, layer, p, mem_k, mem_v, rope_t, tm, tq):
    w_u, w_za, w_cq, w_ckv, w_kr, w_zb, w_qm, w_zm = _cols(
        p["ev_w_in"][i], (S5_WIDTH, S5_WIDTH, MLA_Q_LORA, MLA_KV_LORA, MLA_ROPE, MLA_WIDTH, MEM_WIDTH, MEM_WIDTH))
    w_kr = jnp.pad(w_kr, ((0, 0), (MLA_NOPE, LANES - MLA_QK)))
    weights = [w.astype(BF16) for w in (w_u, w_za, w_zb, w_qm, w_zm, w_cq, w_ckv, w_kr)]
    u, za, zb, qm, zm, cq, ckv, kr = _rms_proj(h, p["norm_g"][layer], weights, [BF16] * 5 + [F32] * 3, tm)

    n_steps = max(1, (s // S5_L - 1).bit_length())
    tables = _s5_tables(p["s5_a_re"][i], p["s5_a_im"][i], p["s5_log_dt"][i], p["s5_b_re"][i], p["s5_b_im"][i],
                        p["s5_c_re"][i], p["s5_c_im"][i], n_steps)
    y_s5 = _s5(u.reshape(b, s, S5_WIDTH), tables, p["s5_d"][i], gb=8).reshape(b * s, S5_WIDTH)

    qt, k, vt = _mla_prep(cq, ckv, kr, rope_t, p["mla_q_a_norm_g"][i], p["mla_w_uq"][i], p["mla_kv_a_norm_g"][i],
                          p["mla_w_ukv"][i], p["mla_q_norm_g"][i], p["mla_k_norm_g"][i], b, s, tq)
    y_b = _mla_attn(qt, k.reshape(b, s, MLA_HEADS * LANES), vt, zb.reshape(b, s, MLA_WIDTH),
                    tq).reshape(b * s, MLA_WIDTH)

    y_m = _mem_attn(qm.reshape(b, s, MEM_WIDTH), zm.reshape(b, s, MEM_WIDTH), mem_k, mem_v,
                    p["mem_q_norm_g"][layer], layer, 2 * tm if s % (2 * tm) == 0 else tm).reshape(b * s, MEM_WIDTH)

    w_out = p["ev_w_out"][i].astype(BF16)
    ws = [w_out[:S5_WIDTH], w_out[S5_WIDTH:S5_WIDTH + MLA_WIDTH], w_out[S5_WIDTH + MLA_WIDTH:]]
    glu = (za, p["s5_glu_w"][i].astype(BF16), p["s5_glu_b"][i].reshape(1, S5_WIDTH))
    return _out_proj(h, [y_s5, y_b, y_m], ws, tm, glu)


def _odd_layer(h, b, s, i, layer, p, mem_k, mem_v, tm):
    w_z, w_xbc, w_dt, w_qm, w_zm = _cols(p["od_w_in"][i], (M2_INNER, M2_CONV_DIM, M2_HEADS, MEM_WIDTH, MEM_WIDTH))
    w_dt = jnp.pad(w_dt, ((0, 0), (0, LANES - M2_HEADS)))
    weights = [w.astype(BF16) for w in (w_z, w_xbc, w_qm, w_zm, w_dt)]
    z, xc, qm, zm, dt_raw = _rms_proj(h, p["norm_g"][layer], weights, [BF16] * 4 + [F32], tm,
                                      conv=(1, p["m2_conv_w"][i], p["m2_conv_b"][i]), seq_len=s)
    y_c = _ssd(xc.reshape(b, s, M2_CONV_DIM), dt_raw.reshape(b, s, LANES), p["m2_dt_bias"][i], p["m2_a_log"][i],
               p["m2_d"][i]).reshape(b * s, M2_INNER)
    y_m = _mem_attn(qm.reshape(b, s, MEM_WIDTH), zm.reshape(b, s, MEM_WIDTH), mem_k, mem_v,
                    p["mem_q_norm_g"][layer], layer, 2 * tm if s % (2 * tm) == 0 else tm).reshape(b * s, MEM_WIDTH)
    w_out = p["od_w_out"][i].astype(BF16)
    return _out_proj(h, [y_c, y_m], [w_out[:M2_INNER], w_out[M2_INNER:]], tm,
                     m2_args=(z, p["m2_norm_g"][i].reshape(1, M2_INNER)))


def _token_tile(s):
    return 512 if s % 512 == 0 else s


def kernel(x, mem, positions, norm_g, mem_norm_g, mem_w_kv, mem_q_norm_g, mem_k_norm_g, ev_w_in, ev_w_out, s5_a_re, s5_a_im, s5_log_dt, s5_b_re, s5_b_im, s5_c_re, s5_c_im, s5_d, s5_glu_w, s5_glu_b, mla_q_a_norm_g, mla_w_uq, mla_kv_a_norm_g, mla_w_ukv, mla_q_norm_g, mla_k_norm_g, od_w_in, od_w_out, m2_conv_w, m2_conv_b, m2_dt_bias, m2_a_log, m2_d, m2_norm_g):
    p = dict(norm_g=norm_g, mem_q_norm_g=mem_q_norm_g, ev_w_in=ev_w_in, ev_w_out=ev_w_out,
             s5_a_re=s5_a_re, s5_a_im=s5_a_im, s5_log_dt=s5_log_dt, s5_b_re=s5_b_re, s5_b_im=s5_b_im,
             s5_c_re=s5_c_re, s5_c_im=s5_c_im, s5_d=s5_d, s5_glu_w=s5_glu_w, s5_glu_b=s5_glu_b,
             mla_q_a_norm_g=mla_q_a_norm_g, mla_w_uq=mla_w_uq, mla_kv_a_norm_g=mla_kv_a_norm_g,
             mla_w_ukv=mla_w_ukv, mla_q_norm_g=mla_q_norm_g, mla_k_norm_g=mla_k_norm_g,
             od_w_in=od_w_in, od_w_out=od_w_out, m2_conv_w=m2_conv_w, m2_conv_b=m2_conv_b,
             m2_dt_bias=m2_dt_bias, m2_a_log=m2_a_log, m2_d=m2_d, m2_norm_g=m2_norm_g)
    b, s, d = x.shape
    tm = _token_tile(s)
    tq = 1024 if s % 1024 == 0 else s
    mem_k, mem_v = _mem_kv(mem, mem_norm_g, mem_w_kv, mem_k_norm_g)
    rope_t = _rope_tables(positions)
    h = x.reshape(b * s, d)
    for layer in range(DEPTH):
        if layer % 2 == 0:
            h = _even_layer(h, b, s, layer // 2, layer, p, mem_k, mem_v, rope_t, tm, tq)
        else:
            h = _odd_layer(h, b, s, layer // 2, layer, p, mem_k, mem_v, tm)
    return h.reshape(b, s, d)
```
